```python
import jax, jax.numpy as jnp
from jax import lax
import numpy as np

D_MODEL = 1024
BATCH = 8
SEQ = 2048
DEPTH = 4

GLA_HEADS = 4
GLA_DK = D_MODEL // 2
GLA_DV = D_MODEL
GLA_HEAD_DK = GLA_DK // GLA_HEADS
GLA_HEAD_DV = GLA_DV // GLA_HEADS
GLA_RANK = 16
GLA_GATE_TAU = 16.0
GLA_CHUNK = 64
LRU_WIDTH = D_MODEL
LRU_BLOCKS = 4
LRU_BLOCK = LRU_WIDTH // LRU_BLOCKS
CONV_WIDTH = 4
LRU_C = 8.0
SB_HEAD_DIM = 128
SB_HEADS = D_MODEL // SB_HEAD_DIM
SB_DIM = SB_HEADS * SB_HEAD_DIM
SB_QBLOCK = 128
N_BRANCHES = 3
D_FF = 4 * D_MODEL
EPS = 1e-6
IN_SIZES = (GLA_DK, GLA_DK, GLA_DV, GLA_DV, GLA_RANK, LRU_WIDTH, LRU_WIDTH, SB_DIM, SB_DIM, SB_DIM, N_BRANCHES * D_MODEL)
IN_COLS = 2 * GLA_DK + 2 * GLA_DV + GLA_RANK + 2 * LRU_WIDTH + 3 * SB_DIM + N_BRANCHES * D_MODEL

kernel_name = "hybrid_gla_rglru_stickbreaking_block"


def rms_norm(x, g):
    x32 = x.astype(jnp.float32)
    y = x32 * lax.rsqrt(jnp.mean(x32 * x32, axis=-1, keepdims=True) + EPS)
    return (y * g.astype(jnp.float32)).astype(x.dtype)


def split_heads(t, n_heads):
    b, s, _ = t.shape
    return t.reshape(b, s, n_heads, -1).transpose(0, 2, 1, 3)


def merge_heads(t):
    b, h, s, d = t.shape
    return t.transpose(0, 2, 1, 3).reshape(b, s, h * d)


def gla_mixer(q, k, v, g_out, a_down, w_up, b_alpha, norm_g):
    f32 = jnp.float32
    b, s, _ = q.shape
    n = s // GLA_CHUNK
    log_alpha = jax.nn.log_sigmoid((a_down @ w_up + b_alpha).astype(f32)) / GLA_GATE_TAU

    def chunks(t):
        return split_heads(t, GLA_HEADS).reshape(b, GLA_HEADS, n, GLA_CHUNK, -1)

    qc = chunks(q.astype(f32)) * (GLA_HEAD_DK ** -0.5)
    kc = chunks(k.astype(f32))
    vc = chunks(v.astype(f32))
    cum = jnp.cumsum(chunks(log_alpha), axis=3)
    last = cum[:, :, :, -1:, :]
    q_dec = qc * jnp.exp(cum)
    k_inv = kc * jnp.exp(-cum)
    k_end = kc * jnp.exp(last - cum)
    pos = jnp.arange(GLA_CHUNK)
    causal = pos[:, None] >= pos[None, :]
    scores = jnp.where(causal, jnp.einsum('bhnti,bhnsi->bhnts', q_dec, k_inv), 0.0)
    o_intra = jnp.einsum('bhnts,bhnsj->bhntj', scores, vc)
    kv_chunk = jnp.einsum('bhnsi,bhnsj->nbhij', k_end, vc)
    decay = jnp.exp(last[:, :, :, 0, :]).transpose(2, 0, 1, 3)

    def step(state, inp):
        d, kv = inp
        return d[..., None] * state + kv, state

    init = jnp.zeros((b, GLA_HEADS, GLA_HEAD_DK, GLA_HEAD_DV), f32)
    _, s_prev = lax.scan(step, init, (decay, kv_chunk))
    o_inter = jnp.einsum('bhnti,nbhij->bhntj', q_dec, s_prev)
    o = (o_intra + o_inter).reshape(b, GLA_HEADS, s, GLA_HEAD_DV)
    o = o * lax.rsqrt(jnp.mean(o * o, axis=-1, keepdims=True) + EPS) * norm_g.astype(f32)
    o = merge_heads(o) * jax.nn.silu(g_out.astype(f32))
    return o.astype(q.dtype)


def rglru_mixer(x_in, gate_in, conv_w, conv_b, w_a, b_a, w_x, b_x, lam):
    f32 = jnp.float32
    xc = lax.conv_general_dilated(
        x_in, conv_w[:, None, :], window_strides=(1,), padding=[(CONV_WIDTH - 1, 0)],
        dimension_numbers=('NWC', 'WIO', 'NWC'), feature_group_count=LRU_WIDTH) + conv_b
    b, s, _ = xc.shape
    xb = xc.reshape(b, s, LRU_BLOCKS, LRU_BLOCK)
    r = jax.nn.sigmoid((jnp.einsum('bthi,hij->bthj', xb, w_a).reshape(b, s, LRU_WIDTH) + b_a).astype(f32))
    i = jax.nn.sigmoid((jnp.einsum('bthi,hij->bthj', xb, w_x).reshape(b, s, LRU_WIDTH) + b_x).astype(f32))
    log_a = -LRU_C * r * jax.nn.softplus(-lam.astype(f32))
    a = jnp.exp(log_a)
    u = jnp.sqrt(-jnp.expm1(2.0 * log_a)) * i * xc.astype(f32)

    def combine(left, right):
        a1, b1 = left
        a2, b2 = right
        return a1 * a2, a2 * b1 + b2

    _, h = lax.associative_scan(combine, (a, u), axis=1)
    y = h * jax.nn.gelu(gate_in.astype(f32))
    return y.astype(x_in.dtype)


def stick_breaking_mixer(q, k, v, q_g, k_g):
    f32 = jnp.float32
    b, s, _ = q.shape

    def qk_norm(t, g):
        t = split_heads(t.astype(f32), SB_HEADS)
        return t * lax.rsqrt(jnp.mean(t * t, axis=-1, keepdims=True) + EPS) * g.astype(f32)

    qh = qk_norm(q, q_g) * (SB_HEAD_DIM ** -0.5)
    kh = qk_norm(k, k_g)
    vh = split_heads(v.astype(f32), SB_HEADS)
    outs = []
    for blk in range(s // SB_QBLOCK):
        q0 = blk * SB_QBLOCK
        q1 = q0 + SB_QBLOCK
        z = jnp.einsum('bhqd,bhkd->bhqk', qh[:, :, q0:q1], kh[:, :, :q1])
        valid = jnp.arange(q1)[None, :] < jnp.arange(q0, q1)[:, None]
        log_keep = jnp.where(valid, jax.nn.log_sigmoid(-z), 0.0)
        after = lax.cumsum(log_keep, axis=3, reverse=True) - log_keep
        w = jnp.where(valid, jnp.exp(jax.nn.log_sigmoid(z) + after), 0.0)
        outs.append(jnp.einsum('bhqk,bhkd->bhqd', w, vh[:, :, :q1]))
    o = jnp.concatenate(outs, axis=2)
    return merge_heads(o).astype(v.dtype)


def hybrid_layer(x, norm_mix_g, w_in, gla_w_up, gla_b_alpha, gla_norm_g,
                 lru_conv_w, lru_conv_b, lru_w_a, lru_b_a, lru_w_x, lru_b_x, lru_lambda,
                 sb_q_norm_g, sb_k_norm_g, w_branch_a, w_branch_b, w_branch_c, b_gate,
                 w_out, norm_mlp_g, w_mlp_up, w_mlp_down):
    b, s, _ = x.shape
    xn = rms_norm(x, norm_mix_g)
    proj = xn @ w_in
    offsets = [int(o) for o in np.cumsum(IN_SIZES)[:-1]]
    gq, gk, gv, gg, gdown, lx, lgate, sq, sk, sv, gate_logits = jnp.split(proj, offsets, axis=-1)
    o_a = gla_mixer(gq, gk, gv, gg, gdown, gla_w_up, gla_b_alpha, gla_norm_g)
    o_b = rglru_mixer(lx, lgate, lru_conv_w, lru_conv_b, lru_w_a, lru_b_a, lru_w_x, lru_b_x, lru_lambda)
    o_c = stick_breaking_mixer(sq, sk, sv, sb_q_norm_g, sb_k_norm_g)
    gates = jax.nn.sigmoid((gate_logits + b_gate).astype(jnp.float32)).astype(x.dtype)
    gates = gates.reshape(b, s, N_BRANCHES, D_MODEL)
    merged = (gates[:, :, 0] * (o_a @ w_branch_a)
              + gates[:, :, 1] * (o_b @ w_branch_b)
              + gates[:, :, 2] * (o_c @ w_branch_c))
    x = x + merged @ w_out
    h = rms_norm(x, norm_mlp_g)
    x = x + jnp.square(jax.nn.relu(h @ w_mlp_up)) @ w_mlp_down
    return x


def setup_inputs(seed: int = 0) -> dict:
    key = jax.random.key(seed)
    ks = jax.random.split(key, 23)
    f32 = jnp.float32

    def dense(k, shape, fan_in, scale=1.0):
        return jax.random.normal(k, shape, f32) * (scale * fan_in ** -0.5)

    def gain(k, shape):
        return 1.0 + 0.02 * jax.random.normal(k, shape, f32)

    def bias(k, shape, scale=0.02):
        return scale * jax.random.normal(k, shape, f32)

    a0 = jax.random.uniform(ks[12], (DEPTH, LRU_WIDTH), f32, minval=0.9, maxval=0.999)
    return {
        'x': jax.random.normal(ks[0], (BATCH, SEQ, D_MODEL), f32),
        'norm_mix_g': gain(ks[1], (DEPTH, D_MODEL)),
        'w_in': dense(ks[2], (DEPTH, D_MODEL, IN_COLS), D_MODEL),
        'gla_w_up': dense(ks[3], (DEPTH, GLA_RANK, GLA_DK), GLA_RANK),
        'gla_b_alpha': bias(ks[4], (DEPTH, GLA_DK), 0.1),
        'gla_norm_g': gain(ks[5], (DEPTH, GLA_HEAD_DV)),
        'lru_conv_w': dense(ks[6], (DEPTH, CONV_WIDTH, LRU_WIDTH), CONV_WIDTH),
        'lru_conv_b': bias(ks[7], (DEPTH, LRU_WIDTH)),
        'lru_w_a': dense(ks[8], (DEPTH, LRU_BLOCKS, LRU_BLOCK, LRU_BLOCK), LRU_BLOCK),
        'lru_b_a': bias(ks[9], (DEPTH, LRU_WIDTH)),
        'lru_w_x': dense(ks[10], (DEPTH, LRU_BLOCKS, LRU_BLOCK, LRU_BLOCK), LRU_BLOCK),
        'lru_b_x': bias(ks[11], (DEPTH, LRU_WIDTH)),
        'lru_lambda': jnp.log(a0) - jnp.log1p(-a0),
        'sb_q_norm_g': gain(ks[13], (DEPTH, SB_HEAD_DIM)),
        'sb_k_norm_g': gain(ks[14], (DEPTH, SB_HEAD_DIM)),
        'w_branch_a': dense(ks[15], (DEPTH, GLA_DV, D_MODEL), GLA_DV),
        'w_branch_b': dense(ks[16], (DEPTH, LRU_WIDTH, D_MODEL), LRU_WIDTH),
        'w_branch_c': dense(ks[17], (DEPTH, SB_DIM, D_MODEL), SB_DIM),
        'b_gate': bias(ks[18], (DEPTH, N_BRANCHES * D_MODEL)),
        'w_out': dense(ks[19], (DEPTH, D_MODEL, D_MODEL), D_MODEL, 0.5),
        'norm_mlp_g': gain(ks[20], (DEPTH, D_MODEL)),
        'w_mlp_up': dense(ks[21], (DEPTH, D_MODEL, D_FF), D_MODEL),
        'w_mlp_down': dense(ks[22], (DEPTH, D_FF, D_MODEL), D_FF, 0.5),
    }


def reference(x, norm_mix_g, w_in, gla_w_up, gla_b_alpha, gla_norm_g,
              lru_conv_w, lru_conv_b, lru_w_a, lru_b_a, lru_w_x, lru_b_x, lru_lambda,
              sb_q_norm_g, sb_k_norm_g, w_branch_a, w_branch_b, w_branch_c, b_gate,
              w_out, norm_mlp_g, w_mlp_up, w_mlp_down):
    for l in range(DEPTH):
        x = hybrid_layer(x, norm_mix_g[l], w_in[l], gla_w_up[l], gla_b_alpha[l], gla_norm_g[l],
                         lru_conv_w[l], lru_conv_b[l], lru_w_a[l], lru_b_a[l], lru_w_x[l], lru_b_x[l],
                         lru_lambda[l], sb_q_norm_g[l], sb_k_norm_g[l], w_branch_a[l], w_branch_b[l],
                         w_branch_c[l], b_gate[l], w_out[l], norm_mlp_g[l], w_mlp_up[l], w_mlp_down[l])
    return x
```

```python
import functools

import jax
import jax.numpy as jnp
from jax import lax
from jax.experimental import pallas as pl
from jax.experimental.pallas import tpu as pltpu

F32 = jnp.float32
BF16 = jnp.bfloat16

D_MODEL = 1024
GLA_HEADS = 4
GLA_DK = 512
GLA_DV = 1024
GLA_HEAD_DK = GLA_DK // GLA_HEADS
GLA_HEAD_DV = GLA_DV // GLA_HEADS
GLA_RANK = 16
GLA_GATE_TAU = 16.0
GLA_CHUNK = 64
LRU_WIDTH = 1024
LRU_BLOCKS = 4
LRU_BLOCK = LRU_WIDTH // LRU_BLOCKS
CONV_WIDTH = 4
LRU_C = 8.0
SB_HEAD_DIM = 128
SB_HEADS = 8
D_FF = 4096
EPS = 1e-6

LANE = 128
SUBLANE = 8
VMEM_LIMIT = 48 * 1024 * 1024

OFF_GQ = 0
OFF_GK = OFF_GQ + GLA_DK
OFF_GV = OFF_GK + GLA_DK
OFF_GG = OFF_GV + GLA_DV
OFF_LX = OFF_GG + GLA_DV
OFF_LG = OFF_LX + LRU_WIDTH
OFF_SQ = OFF_LG + LRU_WIDTH
OFF_SK = OFF_SQ + D_MODEL
OFF_SV = OFF_SK + D_MODEL
OFF_GATE = OFF_SV + D_MODEL
PROJ_COLS = OFF_GATE + 3 * D_MODEL

NT_DIMS = (((1,), (1,)), ((), ()))
TN_DIMS = (((0,), (0,)), ((), ()))


def _params(*sem):
    return pltpu.CompilerParams(dimension_semantics=sem, vmem_limit_bytes=VMEM_LIMIT)


def _softplus(z):
    return jnp.maximum(z, 0.0) + jnp.log1p(jnp.exp(-jnp.abs(z)))


def _sigmoid(z):
    return 1.0 / (1.0 + jnp.exp(-z))


def _in_proj_kernel(x_ref, g_ref, w_ref, wad_ref, proj_ref, ad_ref, xn_ref):
    @pl.when(pl.program_id(1) == 0)
    def _():
        x = x_ref[...]
        xn = x * lax.rsqrt(jnp.mean(x * x, axis=-1, keepdims=True) + EPS) * g_ref[...]
        xnb = xn.astype(BF16)
        xn_ref[...] = xnb
        ad_ref[...] = jnp.dot(xnb, wad_ref[...], preferred_element_type=F32)

    proj_ref[...] = jnp.dot(xn_ref[...], w_ref[...], preferred_element_type=F32).astype(BF16)


def _in_proj(x2, g, w_main, w_ad, tm=1024, tn=1024):
    n = x2.shape[0]
    return pl.pallas_call(
        _in_proj_kernel,
        grid=(n // tm, PROJ_COLS // tn),
        in_specs=[
            pl.BlockSpec((tm, D_MODEL), lambda i, j: (i, 0)),
            pl.BlockSpec((1, D_MODEL), lambda i, j: (0, 0)),
            pl.BlockSpec((D_MODEL, tn), lambda i, j: (0, j)),
            pl.BlockSpec((D_MODEL, LANE), lambda i, j: (0, 0)),
        ],
        out_specs=[
            pl.BlockSpec((tm, tn), lambda i, j: (i, j)),
            pl.BlockSpec((tm, LANE), lambda i, j: (i, 0)),
        ],
        out_shape=[
            jax.ShapeDtypeStruct((n, PROJ_COLS), BF16),
            jax.ShapeDtypeStruct((n, LANE), F32),
        ],
        scratch_shapes=[pltpu.VMEM((tm, D_MODEL), BF16)],
        compiler_params=_params("parallel", "arbitrary"),
        name="in_proj",
    )(x2, g, w_main, w_ad)


def _gla_kernel(q_ref, k_ref, v_ref, g_ref, ad_ref, wup_ref, ba_ref, ng_ref, o_ref,
                st_ref, la_ref):
    seq = q_ref.shape[1]
    c = GLA_CHUNK
    logits = jnp.dot(ad_ref[0].astype(BF16), wup_ref[...], preferred_element_type=F32) + ba_ref[...]
    la_ref[...] = -_softplus(-logits) * (1.0 / GLA_GATE_TAU)
    st_ref[...] = jnp.zeros_like(st_ref)

    row = lax.broadcasted_iota(jnp.int32, (c, c), 0)
    col = lax.broadcasted_iota(jnp.int32, (c, c), 1)
    causal = row >= col
    tri = jnp.where(causal, 1.0, 0.0).astype(BF16)
    scale = GLA_HEAD_DK ** -0.5

    def body(ci, carry):
        r0 = pl.multiple_of(ci * c, c)
        la = la_ref[pl.ds(r0, c), :]
        la_hi = la.astype(BF16)
        la_lo = (la - la_hi.astype(F32)).astype(BF16)
        cum = (jnp.dot(tri, la_hi, preferred_element_type=F32)
               + jnp.dot(tri, la_lo, preferred_element_type=F32))
        last = cum[c - 1:c, :]
        q = q_ref[0, pl.ds(r0, c), :].astype(F32) * scale
        k = k_ref[0, pl.ds(r0, c), :].astype(F32)
        v = v_ref[0, pl.ds(r0, c), :]
        q_dec = (q * jnp.exp(cum)).astype(BF16)
        k_inv = (k * jnp.exp(-cum)).astype(BF16)
        k_end = (k * jnp.exp(last - cum)).astype(BF16)
        s = lax.dot_general(q_dec, k_inv, NT_DIMS, preferred_element_type=F32)
        s = jnp.where(causal, s, 0.0).astype(BF16)
        st = st_ref[...]
        o = (jnp.dot(s, v, preferred_element_type=F32)
             + lax.dot_general(q_dec, st.astype(BF16), NT_DIMS, preferred_element_type=F32))
        st_ref[...] = st * jnp.exp(last) + lax.dot_general(
            v, k_end, TN_DIMS, preferred_element_type=F32)
        o = o * lax.rsqrt(jnp.mean(o * o, axis=-1, keepdims=True) + EPS) * ng_ref[...]
        g = g_ref[0, pl.ds(r0, c), :].astype(F32)
        o_ref[0, pl.ds(r0, c), :] = (o * (g * _sigmoid(g))).astype(BF16)
        return carry

    lax.fori_loop(0, seq // c, body, 0)


def _gla(proj3, ad3, wup, ba, ng):
    b, seq, _ = proj3.shape
    dk, dv = GLA_HEAD_DK, GLA_HEAD_DV
    return pl.pallas_call(
        _gla_kernel,
        grid=(b, GLA_HEADS),
        in_specs=[
            pl.BlockSpec((1, seq, dk), lambda i, h: (i, 0, OFF_GQ // dk + h)),
            pl.BlockSpec((1, seq, dk), lambda i, h: (i, 0, OFF_GK // dk + h)),
            pl.BlockSpec((1, seq, dv), lambda i, h: (i, 0, OFF_GV // dv + h)),
            pl.BlockSpec((1, seq, dv), lambda i, h: (i, 0, OFF_GG // dv + h)),
            pl.BlockSpec((1, seq, LANE), lambda i, h: (i, 0, 0)),
            pl.BlockSpec((LANE, dk), lambda i, h: (0, h)),
            pl.BlockSpec((1, dk), lambda i, h: (0, h)),
            pl.BlockSpec((1, dv), lambda i, h: (0, 0)),
        ],
        out_specs=pl.BlockSpec((1, seq, dv), lambda i, h: (i, 0, h)),
        out_shape=jax.ShapeDtypeStruct((b, seq, GLA_DV), BF16),
        scratch_shapes=[pltpu.VMEM((dv, dk), F32), pltpu.VMEM((seq, dk), F32)],
        compiler_params=_params("parallel", "parallel"),
        name="gla",
    )(proj3, proj3, proj3, proj3, ad3, wup, ba, ng)


def _gelu_tanh(x):
    return 0.5 * x * (1.0 + jnp.tanh(0.7978845608028654 * (x + 0.044715 * (x * x * x))))


def _lru_kernel(x_ref, gate_ref, cw_ref, cb_ref, wa_ref, ba_ref, wx_ref, bx_ref, lam_ref, o_ref,
                a_ref, u_ref):
    seq = x_ref.shape[1]
    x = x_ref[0].astype(F32)
    t_idx = lax.broadcasted_iota(jnp.int32, x.shape, 0)
    xc = x * cw_ref[CONV_WIDTH - 1:CONV_WIDTH, :] + cb_ref[...]
    for s in range(1, CONV_WIDTH):
        shifted = jnp.where(t_idx >= s, pltpu.roll(x, s, 0), 0.0)
        xc = xc + shifted * cw_ref[CONV_WIDTH - 1 - s:CONV_WIDTH - s, :]
    xcb = xc.astype(BF16)
    r = _sigmoid(jnp.dot(xcb, wa_ref[0], preferred_element_type=F32) + ba_ref[...])
    gi = _sigmoid(jnp.dot(xcb, wx_ref[0], preferred_element_type=F32) + bx_ref[...])
    log_a = (-LRU_C) * r * _softplus(-lam_ref[...])
    a = jnp.exp(log_a)
    a_ref[...] = a
    u_ref[...] = jnp.sqrt(1.0 - a * a) * gi * xc

    nrow = SUBLANE
    sub = lax.broadcasted_iota(jnp.int32, (nrow, x.shape[1]), 0)

    def body(i, h):
        r0 = pl.multiple_of(i * nrow, nrow)
        av = a_ref[pl.ds(r0, nrow), :]
        uv = u_ref[pl.ds(r0, nrow), :]
        for s in (1, 2, 4):
            a_sh = jnp.where(sub >= s, pltpu.roll(av, s, 0), 1.0)
            u_sh = jnp.where(sub >= s, pltpu.roll(uv, s, 0), 0.0)
            uv = av * u_sh + uv
            av = av * a_sh
        hv = uv + av * h
        u_ref[pl.ds(r0, nrow), :] = hv
        return hv[nrow - 1:nrow, :]

    lax.fori_loop(0, seq // nrow, body, jnp.zeros((1, x.shape[1]), F32), unroll=4)
    o_ref[0] = (u_ref[...] * _gelu_tanh(gate_ref[0].astype(F32))).astype(BF16)


def _lru(proj3, cw, cb, wa, ba, wx, bx, lam):
    b, seq, _ = proj3.shape
    cblk = LRU_BLOCK
    vec = pl.BlockSpec((1, cblk), lambda i, h: (0, h))
    return pl.pallas_call(
        _lru_kernel,
        grid=(b, LRU_BLOCKS),
        in_specs=[
            pl.BlockSpec((1, seq, cblk), lambda i, h: (i, 0, OFF_LX // cblk + h)),
            pl.BlockSpec((1, seq, cblk), lambda i, h: (i, 0, OFF_LG // cblk + h)),
            pl.BlockSpec((CONV_WIDTH, cblk), lambda i, h: (0, h)),
            vec,
            pl.BlockSpec((1, cblk, cblk), lambda i, h: (h, 0, 0)),
            vec,
            pl.BlockSpec((1, cblk, cblk), lambda i, h: (h, 0, 0)),
            vec,
            vec,
        ],
        out_specs=pl.BlockSpec((1, seq, cblk), lambda i, h: (i, 0, h)),
        out_shape=jax.ShapeDtypeStruct((b, seq, LRU_WIDTH), BF16),
        scratch_shapes=[pltpu.VMEM((seq, cblk), F32), pltpu.VMEM((seq, cblk), F32)],
        compiler_params=_params("parallel", "parallel"),
        name="rglru",
    )(proj3, proj3, cw, cb, wa, ba, wx, bx, lam)


def _qk_norm(t, g):
    return t * lax.rsqrt(jnp.mean(t * t, axis=-1, keepdims=True) + EPS) * g


def _sb_kernel(q_ref, k_ref, v_ref, qg_ref, kg_ref, o_ref, kn_ref, tri_ref):
    tq = q_ref.shape[1]
    tk = tq
    qi = pl.program_id(2)
    row = lax.broadcasted_iota(jnp.int32, (tq, tk), 0)
    col = lax.broadcasted_iota(jnp.int32, (tq, tk), 1)

    @pl.when(qi == 0)
    def _():
        kn_ref[...] = _qk_norm(k_ref[0].astype(F32), kg_ref[...]).astype(BF16)
        tri_ref[...] = jnp.where(row > col, 1.0, 0.0).astype(BF16)

    q = (_qk_norm(q_ref[0].astype(F32), qg_ref[...]) * (SB_HEAD_DIM ** -0.5)).astype(BF16)
    tri = tri_ref[...]

    def block(j, acc, carry, mask):
        k0 = pl.multiple_of(j * tk, tk)
        z = lax.dot_general(q, kn_ref[pl.ds(k0, tk), :], NT_DIMS, preferred_element_type=F32)
        lk = -_softplus(z)
        if mask is not None:
            lk = jnp.where(mask, lk, 0.0)
        lk_hi = lk.astype(BF16)
        lk_lo = (lk - lk_hi.astype(F32)).astype(BF16)
        after = (jnp.dot(lk_hi, tri, preferred_element_type=F32)
                 + jnp.dot(lk_lo, tri, preferred_element_type=F32)) + carry
        w = jnp.exp(z + lk + after)
        if mask is not None:
            w = jnp.where(mask, w, 0.0)
        acc = acc + jnp.dot(w.astype(BF16), v_ref[0, pl.ds(k0, tk), :], preferred_element_type=F32)
        carry = carry + jnp.sum(lk, axis=-1, keepdims=True)
        return acc, carry

    acc, carry = block(qi, jnp.zeros((tq, SB_HEAD_DIM), F32), jnp.zeros((tq, 1), F32), col < row)

    def body(t, ac):
        return block(qi - 1 - t, ac[0], ac[1], None)

    acc, carry = lax.fori_loop(0, qi, body, (acc, carry))
    o_ref[0] = acc.astype(BF16)


def _sb(proj3, qg, kg, tq=256):
    b, seq, _ = proj3.shape
    d = SB_HEAD_DIM
    return pl.pallas_call(
        _sb_kernel,
        grid=(b, SB_HEADS, seq // tq),
        in_specs=[
            pl.BlockSpec((1, tq, d), lambda i, h, s: (i, s, OFF_SQ // d + h)),
            pl.BlockSpec((1, seq, d), lambda i, h, s: (i, 0, OFF_SK // d + h)),
            pl.BlockSpec((1, seq, d), lambda i, h, s: (i, 0, OFF_SV // d + h)),
            pl.BlockSpec((1, d), lambda i, h, s: (0, 0)),
            pl.BlockSpec((1, d), lambda i, h, s: (0, 0)),
        ],
        out_specs=pl.BlockSpec((1, tq, d), lambda i, h, s: (i, s, h)),
        out_shape=jax.ShapeDtypeStruct((b, seq, SB_HEADS * d), BF16),
        scratch_shapes=[pltpu.VMEM((seq, d), BF16), pltpu.VMEM((tq, tq), BF16)],
        compiler_params=_params("parallel", "parallel", "arbitrary"),
        name="stickbreak",
    )(proj3, proj3, proj3, qg, kg)


def _merge_kernel(x_ref, oa_ref, ob_ref, oc_ref, ga_ref, gb_ref, gc_ref, bg_ref,
                  wa_ref, wb_ref, wc_ref, wo_ref, o_ref):
    def branch(o_r, w_r, gl_r, idx):
        gate = _sigmoid(gl_r[...].astype(F32) + bg_ref[:, idx * D_MODEL:(idx + 1) * D_MODEL])
        return gate * jnp.dot(o_r[...], w_r[...], preferred_element_type=F32)

    merged = (branch(oa_ref, wa_ref, ga_ref, 0) + branch(ob_ref, wb_ref, gb_ref, 1)
              + branch(oc_ref, wc_ref, gc_ref, 2))
    o_ref[...] = x_ref[...] + jnp.dot(merged.astype(BF16), wo_ref[...], preferred_element_type=F32)


def _merge(x2, oa, ob, oc, proj, bg, wa, wb, wc, wo, tm=512):
    n = x2.shape[0]
    rows = pl.BlockSpec((tm, D_MODEL), lambda i: (i, 0))
    wspec = pl.BlockSpec((D_MODEL, D_MODEL), lambda i: (0, 0))
    gate_blk = OFF_GATE // D_MODEL

    def gspec(idx):
        return pl.BlockSpec((tm, D_MODEL), lambda i: (i, gate_blk + idx))

    return pl.pallas_call(
        _merge_kernel,
        grid=(n // tm,),
        in_specs=[rows, rows, rows, rows, gspec(0), gspec(1), gspec(2),
                  pl.BlockSpec((1, 3 * D_MODEL), lambda i: (0, 0)),
                  wspec, wspec, wspec, wspec],
        out_specs=rows,
        out_shape=jax.ShapeDtypeStruct((n, D_MODEL), F32),
        compiler_params=_params("parallel"),
        name="merge_out",
    )(x2, oa, ob, oc, proj, proj, proj, bg, wa, wb, wc, wo)


def _mlp_kernel(x_ref, g_ref, wu_ref, wd_ref, o_ref, hn_ref):
    @pl.when(pl.program_id(1) == 0)
    def _():
        x = x_ref[...]
        hn = x * lax.rsqrt(jnp.mean(x * x, axis=-1, keepdims=True) + EPS) * g_ref[...]
        hn_ref[...] = hn.astype(BF16)
        o_ref[...] = x

    up = jnp.dot(hn_ref[...], wu_ref[...], preferred_element_type=F32)
    act = jnp.square(jnp.maximum(up, 0.0)).astype(BF16)
    o_ref[...] += jnp.dot(act, wd_ref[...], preferred_element_type=F32)


def _mlp(x2, g, wu, wd, tm=1024, tf=1024):
    n = x2.shape[0]
    return pl.pallas_call(
        _mlp_kernel,
        grid=(n // tm, D_FF // tf),
        in_specs=[
            pl.BlockSpec((tm, D_MODEL), lambda i, f: (i, 0)),
            pl.BlockSpec((1, D_MODEL), lambda i, f: (0, 0)),
            pl.BlockSpec((D_MODEL, tf), lambda i, f: (0, f)),
            pl.BlockSpec((tf, D_MODEL), lambda i, f: (f, 0)),
        ],
        out_specs=pl.BlockSpec((tm, D_MODEL), lambda i, f: (i, 0)),
        out_shape=jax.ShapeDtypeStruct((n, D_MODEL), F32),
        scratch_shapes=[pltpu.VMEM((tm, D_MODEL), BF16)],
        compiler_params=_params("parallel", "arbitrary"),
        name="mlp",
    )(x2, g, wu, wd)


def kernel(x, norm_mix_g, w_in, gla_w_up, gla_b_alpha, gla_norm_g, lru_conv_w, lru_conv_b, lru_w_a,
           lru_b_a, lru_w_x, lru_b_x, lru_lambda, sb_q_norm_g, sb_k_norm_g, w_branch_a, w_branch_b,
           w_branch_c, b_gate, w_out, norm_mlp_g, w_mlp_up, w_mlp_down):
    b, seq, d = x.shape
    depth = w_in.shape[0]
    ad0 = 2 * GLA_DK + 2 * GLA_DV
    w_main = jnp.concatenate([w_in[:, :, :ad0], w_in[:, :, ad0 + GLA_RANK:]], axis=-1).astype(BF16)
    w_ad = jnp.pad(w_in[:, :, ad0:ad0 + GLA_RANK], ((0, 0), (0, 0), (0, LANE - GLA_RANK))).astype(BF16)
    wup = jnp.pad(gla_w_up, ((0, 0), (0, LANE - GLA_RANK), (0, 0))).astype(BF16)
    wa_l, wx_l = lru_w_a.astype(BF16), lru_w_x.astype(BF16)
    wba, wbb, wbc = w_branch_a.astype(BF16), w_branch_b.astype(BF16), w_branch_c.astype(BF16)
    wo, wu, wd = w_out.astype(BF16), w_mlp_up.astype(BF16), w_mlp_down.astype(BF16)

    def vec(p, l):
        return p[l][None, :]

    x2 = x.reshape(b * seq, d)
    for l in range(depth):
        proj, ad = _in_proj(x2, vec(norm_mix_g, l), w_main[l], w_ad[l])
        proj3 = proj.reshape(b, seq, PROJ_COLS)
        ad3 = ad.reshape(b, seq, LANE)
        o_a = _gla(proj3, ad3, wup[l], vec(gla_b_alpha, l), vec(gla_norm_g, l))
        o_b = _lru(proj3, lru_conv_w[l], vec(lru_conv_b, l), wa_l[l], vec(lru_b_a, l),
                   wx_l[l], vec(lru_b_x, l), vec(lru_lambda, l))
        o_c = _sb(proj3, vec(sb_q_norm_g, l), vec(sb_k_norm_g, l))
        x2 = _merge(x2, o_a.reshape(b * seq, d), o_b.reshape(b * seq, d), o_c.reshape(b * seq, d),
                    proj, vec(b_gate, l), wba[l], wbb[l], wbc[l], wo[l])
        x2 = _mlp(x2, vec(norm_mlp_g, l), wu[l], wd[l])
    return x2.reshape(b, seq, d)
```

```python
import functools

import jax
import jax.numpy as jnp
from jax import lax
from jax.experimental import pallas as pl
from jax.experimental.pallas import tpu as pltpu

F32 = jnp.float32
BF16 = jnp.bfloat16

D_MODEL = 1024
GLA_HEADS = 4
GLA_DK = 512
GLA_DV = 1024
GLA_HEAD_DK = GLA_DK // GLA_HEADS
GLA_HEAD_DV = GLA_DV // GLA_HEADS
GLA_RANK = 16
GLA_GATE_TAU = 16.0
GLA_CHUNK = 64
GLA_UNROLL = 8
LRU_WIDTH = 1024
LRU_BLOCKS = 4
LRU_BLOCK = LRU_WIDTH // LRU_BLOCKS
CONV_WIDTH = 4
LRU_C = 8.0
LRU_BLOCKS_PER_STEP = 2
LRU_SCAN_UNROLL = 8
SB_HEAD_DIM = 128
SB_HEADS = 8
SB_TILE = 128
SB_WINDOW = 256
SB_TILES_PER_STEP = 2
SB_GROUP = 8
SB_DEAD = -60.0
SB_MASKED = -1e30
D_FF = 4096
EPS = 1e-6

LANE = 128
SUBLANE = 8
VMEM_LIMIT = 48 * 1024 * 1024

OFF_GQ = 0
OFF_GK = OFF_GQ + GLA_DK
OFF_GV = OFF_GK + GLA_DK
OFF_GG = OFF_GV + GLA_DV
OFF_LX = OFF_GG + GLA_DV
OFF_LG = OFF_LX + LRU_WIDTH
OFF_SQ = OFF_LG + LRU_WIDTH
OFF_SK = OFF_SQ + D_MODEL
OFF_SV = OFF_SK + D_MODEL
OFF_GATE = OFF_SV + D_MODEL
PROJ_COLS = OFF_GATE + 3 * D_MODEL

NT_DIMS = (((1,), (1,)), ((), ()))
TN_DIMS = (((0,), (0,)), ((), ()))


def _params(*sem):
    return pltpu.CompilerParams(dimension_semantics=sem, vmem_limit_bytes=VMEM_LIMIT)


def _softplus(z):
    return jnp.maximum(z, 0.0) + jnp.log(1.0 + jnp.exp(-jnp.abs(z)))


def _sigmoid(z):
    return 0.5 * jnp.tanh(0.5 * z) + 0.5


def _in_proj_kernel(x_ref, g_ref, w_ref, wad_ref, proj_ref, ad_ref, xn_ref):
    @pl.when(pl.program_id(1) == 0)
    def _():
        x = x_ref[...]
        xn = x * lax.rsqrt(jnp.mean(x * x, axis=-1, keepdims=True) + EPS) * g_ref[...]
        xnb = xn.astype(BF16)
        xn_ref[...] = xnb
        ad_ref[...] = jnp.dot(xnb, wad_ref[...], preferred_element_type=F32)

    proj_ref[...] = jnp.dot(xn_ref[...], w_ref[...], preferred_element_type=F32).astype(BF16)


def _in_proj(x2, g, w_main, w_ad, tm=2048, tn=1024):
    n = x2.shape[0]
    return pl.pallas_call(
        _in_proj_kernel,
        grid=(n // tm, PROJ_COLS // tn),
        in_specs=[
            pl.BlockSpec((tm, D_MODEL), lambda i, j: (i, 0)),
            pl.BlockSpec((1, D_MODEL), lambda i, j: (0, 0)),
            pl.BlockSpec((D_MODEL, tn), lambda i, j: (0, j)),
            pl.BlockSpec((D_MODEL, LANE), lambda i, j: (0, 0)),
        ],
        out_specs=[
            pl.BlockSpec((tm, tn), lambda i, j: (i, j)),
            pl.BlockSpec((tm, LANE), lambda i, j: (i, 0)),
        ],
        out_shape=[
            jax.ShapeDtypeStruct((n, PROJ_COLS), BF16),
            jax.ShapeDtypeStruct((n, LANE), F32),
        ],
        scratch_shapes=[pltpu.VMEM((tm, D_MODEL), BF16)],
        compiler_params=_params("parallel", "arbitrary"),
        name="in_proj",
    )(x2, g, w_main, w_ad)


def _gla_kernel(q_ref, k_ref, v_ref, g_ref, ad_ref, wup_ref, ba_ref, ng_ref, o_ref,
                la_ref, qd_ref, oi_ref, kv_ref, dec_ref):
    seq = q_ref.shape[1]
    c = GLA_CHUNK
    logits = jnp.dot(ad_ref[0].astype(BF16), wup_ref[...], preferred_element_type=F32) + ba_ref[...]
    la_ref[...] = -_softplus(-logits) * (1.0 / GLA_GATE_TAU)

    row = lax.broadcasted_iota(jnp.int32, (c, c), 0)
    col = lax.broadcasted_iota(jnp.int32, (c, c), 1)
    causal = row >= col
    tri = jnp.where(causal, 1.0, 0.0).astype(BF16)
    scale = GLA_HEAD_DK ** -0.5

    def chunk_local(gi, carry):
        rows = [pl.multiple_of((gi * GLA_UNROLL + u) * c, c) for u in range(GLA_UNROLL)]
        loaded = [(la_ref[pl.ds(r0, c), :], q_ref[0, pl.ds(r0, c), :], k_ref[0, pl.ds(r0, c), :],
                   v_ref[0, pl.ds(r0, c), :]) for r0 in rows]
        cums = []
        for la, _, _, _ in loaded:
            la_hi = la.astype(BF16)
            la_lo = (la - la_hi.astype(F32)).astype(BF16)
            cums.append(jnp.dot(tri, la_hi, preferred_element_type=F32)
                        + jnp.dot(tri, la_lo, preferred_element_type=F32))
        scaled = []
        for cum, (_, q, k, _) in zip(cums, loaded):
            last = cum[c - 1:c, :]
            k = k.astype(F32)
            scaled.append(((q.astype(F32) * scale * jnp.exp(cum)).astype(BF16),
                           (k * jnp.exp(-cum)).astype(BF16),
                           (k * jnp.exp(last - cum)).astype(BF16),
                           jnp.broadcast_to(jnp.exp(last), (SUBLANE, GLA_HEAD_DK))))
        scores = [lax.dot_general(q_dec, k_inv, NT_DIMS, preferred_element_type=F32)
                  for q_dec, k_inv, _, _ in scaled]
        scores = [jnp.where(causal, s, 0.0).astype(BF16) for s in scores]
        results = [(q_dec, jnp.dot(s, v, preferred_element_type=F32),
                    lax.dot_general(v, k_end, TN_DIMS, preferred_element_type=F32), dec)
                   for s, (q_dec, _, k_end, dec), (_, _, _, v) in zip(scores, scaled, loaded)]
        for u, (q_dec, o_intra, kv, dec) in enumerate(results):
            qd_ref[pl.ds(rows[u], c), :] = q_dec
            oi_ref[pl.ds(rows[u], c), :] = o_intra
            kv_ref[gi * GLA_UNROLL + u] = kv
            dec_ref[gi * GLA_UNROLL + u] = dec
        return carry

    lax.fori_loop(0, seq // (c * GLA_UNROLL), chunk_local, 0)

    def chunk_state(gi, st):
        rows = [pl.multiple_of((gi * GLA_UNROLL + u) * c, c) for u in range(GLA_UNROLL)]
        loaded = [(oi_ref[pl.ds(r0, c), :], qd_ref[pl.ds(r0, c), :], g_ref[0, pl.ds(r0, c), :],
                   kv_ref[gi * GLA_UNROLL + u], dec_ref[gi * GLA_UNROLL + u])
                  for u, r0 in enumerate(rows)]
        outs = []
        for o_intra, q_dec, g, kv, dec in loaded:
            o = o_intra + lax.dot_general(q_dec, st.astype(BF16), NT_DIMS,
                                          preferred_element_type=F32)
            st = st * dec[0:1, :] + kv
            o = o * lax.rsqrt(jnp.mean(o * o, axis=-1, keepdims=True) + EPS) * ng_ref[...]
            g = g.astype(F32)
            outs.append((o * (g * _sigmoid(g))).astype(BF16))
        for u, o in enumerate(outs):
            o_ref[0, pl.ds(rows[u], c), :] = o
        return st

    lax.fori_loop(0, seq // (c * GLA_UNROLL), chunk_state,
                  jnp.zeros((GLA_HEAD_DV, GLA_HEAD_DK), F32))


def _gla(proj3, ad3, wup, ba, ng):
    b, seq, _ = proj3.shape
    dk, dv = GLA_HEAD_DK, GLA_HEAD_DV
    return pl.pallas_call(
        _gla_kernel,
        grid=(b, GLA_HEADS),
        in_specs=[
            pl.BlockSpec((1, seq, dk), lambda i, h: (i, 0, OFF_GQ // dk + h)),
            pl.BlockSpec((1, seq, dk), lambda i, h: (i, 0, OFF_GK // dk + h)),
            pl.BlockSpec((1, seq, dv), lambda i, h: (i, 0, OFF_GV // dv + h)),
            pl.BlockSpec((1, seq, dv), lambda i, h: (i, 0, OFF_GG // dv + h)),
            pl.BlockSpec((1, seq, LANE), lambda i, h: (i, 0, 0)),
            pl.BlockSpec((LANE, dk), lambda i, h: (0, h)),
            pl.BlockSpec((1, dk), lambda i, h: (0, h)),
            pl.BlockSpec((1, dv), lambda i, h: (0, 0)),
        ],
        out_specs=pl.BlockSpec((1, seq, dv), lambda i, h: (i, 0, h)),
        out_shape=jax.ShapeDtypeStruct((b, seq, GLA_DV), BF16),
        scratch_shapes=[
            pltpu.VMEM((seq, dk), F32),
            pltpu.VMEM((seq, dk), BF16),
            pltpu.VMEM((seq, dv), F32),
            pltpu.VMEM((seq // GLA_CHUNK, dv, dk), F32),
            pltpu.VMEM((seq // GLA_CHUNK, SUBLANE, dk), F32),
        ],
        compiler_params=_params("parallel", "parallel"),
        name="gla",
    )(proj3, proj3, proj3, proj3, ad3, wup, ba, ng)


def _gelu_tanh(x):
    return 0.5 * x * (1.0 + jnp.tanh(0.7978845608028654 * (x + 0.044715 * (x * x * x))))


def _lru_kernel(x_ref, gate_ref, cw_ref, cb_ref, wa_ref, ba_ref, wx_ref, bx_ref, lam_ref, o_ref,
                a_ref, u_ref):
    seq = x_ref.shape[1]
    x = x_ref[0].astype(F32)
    t_idx = lax.broadcasted_iota(jnp.int32, x.shape, 0)
    xc = x * cw_ref[CONV_WIDTH - 1:CONV_WIDTH, :] + cb_ref[...]
    for s in range(1, CONV_WIDTH):
        shifted = jnp.where(t_idx >= s, pltpu.roll(x, s, 0), 0.0)
        xc = xc + shifted * cw_ref[CONV_WIDTH - 1 - s:CONV_WIDTH - s, :]
    xcb = xc.astype(BF16)
    neg_c_sp = (-LRU_C) * _softplus(-lam_ref[...])
    for blk in range(wa_ref.shape[0]):
        cs = slice(blk * LRU_BLOCK, (blk + 1) * LRU_BLOCK)
        xb = xcb[:, cs]
        r = _sigmoid(jnp.dot(xb, wa_ref[blk], preferred_element_type=F32) + ba_ref[:, cs])
        gi = _sigmoid(jnp.dot(xb, wx_ref[blk], preferred_element_type=F32) + bx_ref[:, cs])
        a = jnp.exp(r * neg_c_sp[:, cs])
        y = 1.0 - a * a
        root = jnp.where(y > 0.0, y * lax.rsqrt(y), 0.0)
        a_ref[:, cs] = a
        u_ref[:, cs] = root * gi * xc[:, cs]

    nrow = SUBLANE
    sub = lax.broadcasted_iota(jnp.int32, (nrow, x.shape[1]), 0)

    def body(gi, h):
        rows = [pl.multiple_of((gi * LRU_SCAN_UNROLL + j) * nrow, nrow)
                for j in range(LRU_SCAN_UNROLL)]
        tiles = [(a_ref[pl.ds(r0, nrow), :], u_ref[pl.ds(r0, nrow), :]) for r0 in rows]
        outs = []
        for av, uv in tiles:
            for s in (1, 2, 4):
                a_sh = jnp.where(sub >= s, pltpu.roll(av, s, 0), 1.0)
                u_sh = jnp.where(sub >= s, pltpu.roll(uv, s, 0), 0.0)
                uv = av * u_sh + uv
                av = av * a_sh
            hv = uv + av * h
            h = hv[nrow - 1:nrow, :]
            outs.append(hv)
        for r0, hv in zip(rows, outs):
            u_ref[pl.ds(r0, nrow), :] = hv
        return h

    lax.fori_loop(0, seq // (nrow * LRU_SCAN_UNROLL), body, jnp.zeros((1, x.shape[1]), F32))
    o_ref[0] = (u_ref[...] * _gelu_tanh(gate_ref[0].astype(F32))).astype(BF16)


def _lru(proj3, cw, cb, wa, ba, wx, bx, lam):
    b, seq, _ = proj3.shape
    nblk = LRU_BLOCKS_PER_STEP
    cblk = nblk * LRU_BLOCK
    vec = pl.BlockSpec((1, cblk), lambda i, h: (0, h))
    return pl.pallas_call(
        _lru_kernel,
        grid=(b, LRU_BLOCKS // nblk),
        in_specs=[
            pl.BlockSpec((1, seq, cblk), lambda i, h: (i, 0, OFF_LX // cblk + h)),
            pl.BlockSpec((1, seq, cblk), lambda i, h: (i, 0, OFF_LG // cblk + h)),
            pl.BlockSpec((CONV_WIDTH, cblk), lambda i, h: (0, h)),
            vec,
            pl.BlockSpec((nblk, LRU_BLOCK, LRU_BLOCK), lambda i, h: (h, 0, 0)),
            vec,
            pl.BlockSpec((nblk, LRU_BLOCK, LRU_BLOCK), lambda i, h: (h, 0, 0)),
            vec,
            vec,
        ],
        out_specs=pl.BlockSpec((1, seq, cblk), lambda i, h: (i, 0, h)),
        out_shape=jax.ShapeDtypeStruct((b, seq, LRU_WIDTH), BF16),
        scratch_shapes=[pltpu.VMEM((seq, cblk), F32), pltpu.VMEM((seq, cblk), F32)],
        compiler_params=_params("parallel", "parallel"),
        name="rglru",
    )(proj3, proj3, cw, cb, wa, ba, wx, bx, lam)


def _qk_norm(t, g):
    return t * lax.rsqrt(jnp.mean(t * t, axis=-1, keepdims=True) + EPS) * g


def _sb_kernel(q_ref, k_ref, v_ref, qg_ref, kg_ref, o_ref, kn_ref, acc_ref, car_ref, bias_ref):
    t, wk, d = SB_TILE, SB_WINDOW, SB_HEAD_DIM
    step = pl.program_id(1)
    row = lax.broadcasted_iota(jnp.int32, (t, wk), 0)
    col = lax.broadcasted_iota(jnp.int32, (t, wk), 1)
    col_minus_row = col - row
    tri = jnp.where(lax.broadcasted_iota(jnp.int32, (wk, wk), 0)
                    > lax.broadcasted_iota(jnp.int32, (wk, wk), 1), 1.0, 0.0).astype(BF16)

    def hs(h):
        return slice(h * d, (h + 1) * d)

    @pl.when(step == 0)
    def _():
        for h in range(SB_HEADS):
            kn_ref[:, hs(h)] = _qk_norm(k_ref[0, :, hs(h)].astype(F32), kg_ref[...]).astype(BF16)

    def block(qh, h, kstart, bias, acc, carry):
        z = lax.dot_general(qh, kn_ref[pl.ds(kstart, wk), hs(h)], NT_DIMS,
                            preferred_element_type=F32) + bias
        lsz = jnp.minimum(z, 0.0) - jnp.log(1.0 + jnp.exp(-jnp.abs(z)))
        lk = lsz - z
        after = jnp.dot(lk.astype(BF16), tri, preferred_element_type=F32) + carry
        w = jnp.exp(lsz + after)
        acc = acc + jnp.dot(w.astype(BF16), v_ref[0, pl.ds(kstart, wk), hs(h)],
                            preferred_element_type=F32)
        carry = carry + jnp.sum(lk, axis=-1, keepdims=True)
        return acc, carry

    def alive(carry):
        return (jnp.max(carry) > SB_DEAD).astype(jnp.int32)

    chains = []
    for s in range(SB_TILES_PER_STEP):
        tile = step * SB_TILES_PER_STEP + s
        kstart = pl.multiple_of(jnp.maximum(tile - 1, 0) * t, t)
        bias_ref[s] = jnp.where(col_minus_row < tile * t - kstart, 0.0, SB_MASKED)
        rs = slice(s * t, (s + 1) * t)
        for h in range(SB_HEADS):
            chains.append((s * SB_HEADS + h, s, h, kstart, rs))

    worst = None
    tails = []
    for g0 in range(0, len(chains), SB_GROUP):
        group = chains[g0:g0 + SB_GROUP]
        qs = [(_qk_norm(q_ref[0, rs, hs(h)].astype(F32), qg_ref[...]) * (d ** -0.5)).astype(BF16)
              for _, _, h, _, rs in group]
        zs = [lax.dot_general(qh, kn_ref[pl.ds(kstart, wk), hs(h)], NT_DIMS,
                              preferred_element_type=F32) + bias_ref[s]
              for qh, (_, s, h, kstart, _) in zip(qs, group)]
        lszs = [jnp.minimum(z, 0.0) - jnp.log(1.0 + jnp.exp(-jnp.abs(z))) for z in zs]
        lks = [lsz - z for lsz, z in zip(lszs, zs)]
        afters = [jnp.dot(lk.astype(BF16), tri, preferred_element_type=F32) for lk in lks]
        carries = [jnp.sum(lk, axis=-1, keepdims=True) for lk in lks]
        ws = [jnp.exp(lsz + after).astype(BF16) for lsz, after in zip(lszs, afters)]
        accs = [jnp.dot(w, v_ref[0, pl.ds(kstart, wk), hs(h)], preferred_element_type=F32)
                for w, (_, _, h, kstart, _) in zip(ws, group)]
        for (c, _, h, kstart, rs), qh, acc, carry in zip(group, qs, accs, carries):
            acc_ref[c] = acc
            car_ref[c] = carry
            o_ref[0, rs, hs(h)] = acc.astype(BF16)
            tails.append((c, qh, h, kstart, rs))
            worst = carry if worst is None else jnp.maximum(worst, carry)
    chains = tails

    @pl.when(jnp.logical_and(alive(worst) > 0, step > 0))
    def _():
        for c, qh, h, kend0, rs in chains:
            def cond(st):
                return jnp.logical_and(st[0] > 0, st[1] > 0)

            def body(st, c=c, qh=qh, h=h):
                kend = st[0]
                kstart = pl.multiple_of(jnp.maximum(kend - wk, 0), t)
                bias = jnp.where(col < kend - kstart, 0.0, SB_MASKED)
                acc, carry = block(qh, h, kstart, bias, acc_ref[c], car_ref[c])
                acc_ref[c] = acc
                car_ref[c] = carry
                return kstart, alive(carry)

            lax.while_loop(cond, body, (kend0, alive(car_ref[c])))
            o_ref[0, rs, hs(h)] = acc_ref[c].astype(BF16)


def _sb(proj3, qg, kg):
    b, seq, _ = proj3.shape
    rows = SB_TILES_PER_STEP * SB_TILE
    w = SB_HEADS * SB_HEAD_DIM
    nchain = SB_TILES_PER_STEP * SB_HEADS
    vec = pl.BlockSpec((1, SB_HEAD_DIM), lambda i, s: (0, 0))
    return pl.pallas_call(
        _sb_kernel,
        grid=(b, seq // rows),
        in_specs=[
            pl.BlockSpec((1, rows, w), lambda i, s: (i, s, OFF_SQ // w)),
            pl.BlockSpec((1, seq, w), lambda i, s: (i, 0, OFF_SK // w)),
            pl.BlockSpec((1, seq, w), lambda i, s: (i, 0, OFF_SV // w)),
            vec, vec,
        ],
        out_specs=pl.BlockSpec((1, rows, w), lambda i, s: (i, s, 0)),
        out_shape=jax.ShapeDtypeStruct((b, seq, w), BF16),
        scratch_shapes=[pltpu.VMEM((seq, w), BF16),
                        pltpu.VMEM((nchain, SB_TILE, SB_HEAD_DIM), F32),
                        pltpu.VMEM((nchain, SB_TILE, 1), F32),
                        pltpu.VMEM((SB_TILES_PER_STEP, SB_TILE, SB_WINDOW), F32)],
        compiler_params=_params("parallel", "arbitrary"),
        name="stickbreak",
    )(proj3, proj3, proj3, qg, kg)


def _merge_kernel(x_ref, oa_ref, ob_ref, oc_ref, ga_ref, gb_ref, gc_ref, bg_ref,
                  wa_ref, wb_ref, wc_ref, wo_ref, o_ref):
    def branch(o_r, w_r, gl_r, idx):
        gate = _sigmoid(gl_r[...].astype(F32) + bg_ref[:, idx * D_MODEL:(idx + 1) * D_MODEL])
        return gate * jnp.dot(o_r[...], w_r[...], preferred_element_type=F32)

    merged = (branch(oa_ref, wa_ref, ga_ref, 0) + branch(ob_ref, wb_ref, gb_ref, 1)
              + branch(oc_ref, wc_ref, gc_ref, 2))
    o_ref[...] = x_ref[...] + jnp.dot(merged.astype(BF16), wo_ref[...], preferred_element_type=F32)


def _merge(x2, oa, ob, oc, proj, bg, wa, wb, wc, wo, tm=512):
    n = x2.shape[0]
    rows = pl.BlockSpec((tm, D_MODEL), lambda i: (i, 0))
    wspec = pl.BlockSpec((D_MODEL, D_MODEL), lambda i: (0, 0))
    gate_blk = OFF_GATE // D_MODEL

    def gspec(idx):
        return pl.BlockSpec((tm, D_MODEL), lambda i: (i, gate_blk + idx))

    return pl.pallas_call(
        _merge_kernel,
        grid=(n // tm,),
        in_specs=[rows, rows, rows, rows, gspec(0), gspec(1), gspec(2),
                  pl.BlockSpec((1, 3 * D_MODEL), lambda i: (0, 0)),
                  wspec, wspec, wspec, wspec],
        out_specs=rows,
        out_shape=jax.ShapeDtypeStruct((n, D_MODEL), F32),
        compiler_params=_params("parallel"),
        name="merge_out",
    )(x2, oa, ob, oc, proj, proj, proj, bg, wa, wb, wc, wo)


def _mlp_kernel(x_ref, g_ref, wu_ref, wd_ref, o_ref, hn_ref):
    @pl.when(pl.program_id(1) == 0)
    def _():
        x = x_ref[...]
        hn = x * lax.rsqrt(jnp.mean(x * x, axis=-1, keepdims=True) + EPS) * g_ref[...]
        hn_ref[...] = hn.astype(BF16)
        o_ref[...] = x

    up = jnp.dot(hn_ref[...], wu_ref[...], preferred_element_type=F32)
    act = jnp.square(jnp.maximum(up, 0.0)).astype(BF16)
    o_ref[...] += jnp.dot(act, wd_ref[...], preferred_element_type=F32)


def _mlp(x2, g, wu, wd, tm=1024, tf=1024):
    n = x2.shape[0]
    return pl.pallas_call(
        _mlp_kernel,
        grid=(n // tm, D_FF // tf),
        in_specs=[
            pl.BlockSpec((tm, D_MODEL), lambda i, f: (i, 0)),
            pl.BlockSpec((1, D_MODEL), lambda i, f: (0, 0)),
            pl.BlockSpec((D_MODEL, tf), lambda i, f: (0, f)),
            pl.BlockSpec((tf, D_MODEL), lambda i, f: (f, 0)),
        ],
        out_specs=pl.BlockSpec((tm, D_MODEL), lambda i, f: (i, 0)),
        out_shape=jax.ShapeDtypeStruct((n, D_MODEL), F32),
        scratch_shapes=[pltpu.VMEM((tm, D_MODEL), BF16)],
        compiler_params=_params("parallel", "arbitrary"),
        name="mlp",
    )(x2, g, wu, wd)


def kernel(x, norm_mix_g, w_in, gla_w_up, gla_b_alpha, gla_norm_g, lru_conv_w, lru_conv_b, lru_w_a,
           lru_b_a, lru_w_x, lru_b_x, lru_lambda, sb_q_norm_g, sb_k_norm_g, w_branch_a, w_branch_b,
           w_branch_c, b_gate, w_out, norm_mlp_g, w_mlp_up, w_mlp_down):
    b, seq, d = x.shape
    depth = w_in.shape[0]
    ad0 = 2 * GLA_DK + 2 * GLA_DV
    w_main = jnp.concatenate([w_in[:, :, :ad0], w_in[:, :, ad0 + GLA_RANK:]], axis=-1).astype(BF16)
    w_ad = jnp.pad(w_in[:, :, ad0:ad0 + GLA_RANK], ((0, 0), (0, 0), (0, LANE - GLA_RANK))).astype(BF16)
    wup = jnp.pad(gla_w_up, ((0, 0), (0, LANE - GLA_RANK), (0, 0))).astype(BF16)
    wa_l, wx_l = lru_w_a.astype(BF16), lru_w_x.astype(BF16)
    wba, wbb, wbc = w_branch_a.astype(BF16), w_branch_b.astype(BF16), w_branch_c.astype(BF16)
    wo, wu, wd = w_out.astype(BF16), w_mlp_up.astype(BF16), w_mlp_down.astype(BF16)

    def vec(p, l):
        return p[l][None, :]

    x2 = x.reshape(b * seq, d)
    for l in range(depth):
        proj, ad = _in_proj(x2, vec(norm_mix_g, l), w_main[l], w_ad[l])
        proj3 = proj.reshape(b, seq, PROJ_COLS)
        ad3 = ad.reshape(b, seq, LANE)
        o_a = _gla(proj3, ad3, wup[l], vec(gla_b_alpha, l), vec(gla_norm_g, l))
        o_b = _lru(proj3, lru_conv_w[l], vec(lru_conv_b, l), wa_l[l], vec(lru_b_a, l),
                   wx_l[l], vec(lru_b_x, l), vec(lru_lambda, l))
        o_c = _sb(proj3, vec(sb_q_norm_g, l), vec(sb_k_norm_g, l))
        x2 = _merge(x2, o_a.reshape(b * seq, d), o_b.reshape(b * seq, d), o_c.reshape(b * seq, d),
                    proj, vec(b_gate, l), wba[l], wbb[l], wbc[l], wo[l])
        x2 = _mlp(x2, vec(norm_mlp_g, l), wu[l], wd[l])
    return x2.reshape(b, seq, d)
```

```python
import jax
import jax.numpy as jnp
from jax import lax
from jax.experimental import pallas as pl
from jax.experimental.pallas import tpu as pltpu

F32 = jnp.float32
BF16 = jnp.bfloat16

D_MODEL = 1024
GLA_HEADS = 4
GLA_DK = 512
GLA_DV = 1024
GLA_HEAD_DK = GLA_DK // GLA_HEADS
GLA_HEAD_DV = GLA_DV // GLA_HEADS
GLA_RANK = 16
GLA_GATE_TAU = 16.0
GLA_CHUNK = 64
GLA_UNROLL = 8
LRU_WIDTH = 1024
LRU_BLOCKS = 4
LRU_BLOCK = LRU_WIDTH // LRU_BLOCKS
CONV_WIDTH = 4
LRU_C = 8.0
LRU_BLOCKS_PER_STEP = 2
LRU_SCAN_UNROLL = 8
SB_HEAD_DIM = 128
SB_HEADS = 8
SB_TILE = 128
SB_WINDOW = 256
SB_TILES_PER_STEP = 2
SB_GROUP = 8
SB_DEAD = -60.0
SB_MASKED = -1e30
D_FF = 4096
EPS = 1e-6

LANE = 128
SUBLANE = 8
VMEM_LIMIT = 48 * 1024 * 1024

OFF_GQ = 0
OFF_GK = OFF_GQ + GLA_DK
OFF_GV = OFF_GK + GLA_DK
OFF_GG = OFF_GV + GLA_DV
OFF_LX = OFF_GG + GLA_DV
OFF_LG = OFF_LX + LRU_WIDTH
OFF_SQ = OFF_LG + LRU_WIDTH
OFF_SK = OFF_SQ + D_MODEL
OFF_SV = OFF_SK + D_MODEL
OFF_GATE = OFF_SV + D_MODEL
PROJ_COLS = OFF_GATE + 3 * D_MODEL

NT_DIMS = (((1,), (1,)), ((), ()))
TN_DIMS = (((0,), (0,)), ((), ()))


def _params(*sem):
    return pltpu.CompilerParams(dimension_semantics=sem, vmem_limit_bytes=VMEM_LIMIT)


def _layer_spec(l, shape, index_map):
    return pl.BlockSpec((None,) + shape, lambda *g: (l,) + index_map(*g))


def _softplus(z):
    return jnp.maximum(z, 0.0) + jnp.log(1.0 + jnp.exp(-jnp.abs(z)))


def _sigmoid(z):
    return 0.5 * jnp.tanh(0.5 * z) + 0.5


def _in_proj_kernel(x_ref, g_ref, w_ref, wad_ref, proj_ref, ad_ref, xn_ref):
    @pl.when(pl.program_id(1) == 0)
    def _():
        x = x_ref[...]
        xn = x * lax.rsqrt(jnp.mean(x * x, axis=-1, keepdims=True) + EPS) * g_ref[...]
        xnb = xn.astype(BF16)
        xn_ref[...] = xnb
        ad_ref[...] = jnp.dot(xnb, wad_ref[...], preferred_element_type=F32)

    proj_ref[...] = jnp.dot(xn_ref[...], w_ref[...], preferred_element_type=F32).astype(BF16)


def _in_proj(l, x2, g, w_main, w_ad, tm=2048, tn=1024):
    n = x2.shape[0]
    return pl.pallas_call(
        _in_proj_kernel,
        grid=(n // tm, PROJ_COLS // tn),
        in_specs=[
            pl.BlockSpec((tm, D_MODEL), lambda i, j: (i, 0)),
            _layer_spec(l, (1, D_MODEL), lambda i, j: (0, 0)),
            _layer_spec(l, (D_MODEL, tn), lambda i, j: (0, j)),
            _layer_spec(l, (D_MODEL, LANE), lambda i, j: (0, 0)),
        ],
        out_specs=[
            pl.BlockSpec((tm, tn), lambda i, j: (i, j)),
            pl.BlockSpec((tm, LANE), lambda i, j: (i, 0)),
        ],
        out_shape=[
            jax.ShapeDtypeStruct((n, PROJ_COLS), BF16),
            jax.ShapeDtypeStruct((n, LANE), F32),
        ],
        scratch_shapes=[pltpu.VMEM((tm, D_MODEL), BF16)],
        compiler_params=_params("parallel", "arbitrary"),
        name="in_proj",
    )(x2, g, w_main, w_ad)


def _gla_kernel(q_ref, k_ref, v_ref, g_ref, ad_ref, wup_ref, ba_ref, ng_ref, o_ref,
                la_ref, qd_ref, oi_ref, kv_ref, dec_ref):
    seq = q_ref.shape[1]
    c = GLA_CHUNK
    logits = jnp.dot(ad_ref[0].astype(BF16), wup_ref[...], preferred_element_type=F32) + ba_ref[...]
    la_ref[...] = -_softplus(-logits) * (1.0 / GLA_GATE_TAU)

    row = lax.broadcasted_iota(jnp.int32, (c, c), 0)
    col = lax.broadcasted_iota(jnp.int32, (c, c), 1)
    causal = row >= col
    tri = jnp.where(causal, 1.0, 0.0).astype(BF16)
    scale = GLA_HEAD_DK ** -0.5

    def chunk_local(gi, carry):
        rows = [pl.multiple_of((gi * GLA_UNROLL + u) * c, c) for u in range(GLA_UNROLL)]
        loaded = [(la_ref[pl.ds(r0, c), :], q_ref[0, pl.ds(r0, c), :], k_ref[0, pl.ds(r0, c), :],
                   v_ref[0, pl.ds(r0, c), :]) for r0 in rows]
        cums = [jnp.dot(tri, la.astype(BF16), preferred_element_type=F32)
                for la, _, _, _ in loaded]
        scaled = []
        for cum, (_, q, k, _) in zip(cums, loaded):
            last = cum[c - 1:c, :]
            k = k.astype(F32)
            scaled.append(((q.astype(F32) * scale * jnp.exp(cum)).astype(BF16),
                           (k * jnp.exp(-cum)).astype(BF16),
                           (k * jnp.exp(last - cum)).astype(BF16),
                           jnp.broadcast_to(jnp.exp(last), (SUBLANE, GLA_HEAD_DK))))
        scores = [lax.dot_general(q_dec, k_inv, NT_DIMS, preferred_element_type=F32)
                  for q_dec, k_inv, _, _ in scaled]
        kvs = [lax.dot_general(v, k_end, TN_DIMS, preferred_element_type=F32)
               for (_, _, k_end, _), (_, _, _, v) in zip(scaled, loaded)]
        scores = [jnp.where(causal, s, 0.0).astype(BF16) for s in scores]
        results = [(q_dec, jnp.dot(s, v, preferred_element_type=F32), kv, dec)
                   for s, kv, (q_dec, _, _, dec), (_, _, _, v) in zip(scores, kvs, scaled, loaded)]
        for u, (q_dec, o_intra, kv, dec) in enumerate(results):
            qd_ref[pl.ds(rows[u], c), :] = q_dec
            oi_ref[pl.ds(rows[u], c), :] = o_intra
            kv_ref[gi * GLA_UNROLL + u] = kv
            dec_ref[gi * GLA_UNROLL + u] = dec
        return carry

    lax.fori_loop(0, seq // (c * GLA_UNROLL), chunk_local, 0)

    def chunk_state(gi, st):
        rows = [pl.multiple_of((gi * GLA_UNROLL + u) * c, c) for u in range(GLA_UNROLL)]
        loaded = [(oi_ref[pl.ds(r0, c), :], qd_ref[pl.ds(r0, c), :], g_ref[0, pl.ds(r0, c), :],
                   kv_ref[gi * GLA_UNROLL + u], dec_ref[gi * GLA_UNROLL + u])
                  for u, r0 in enumerate(rows)]
        outs = []
        for o_intra, q_dec, g, kv, dec in loaded:
            o = o_intra + lax.dot_general(q_dec, st.astype(BF16), NT_DIMS,
                                          preferred_element_type=F32)
            st = st * dec[0:1, :] + kv
            o = o * lax.rsqrt(jnp.mean(o * o, axis=-1, keepdims=True) + EPS) * ng_ref[...]
            g = g.astype(F32)
            outs.append((o * (g * _sigmoid(g))).astype(BF16))
        for u, o in enumerate(outs):
            o_ref[0, pl.ds(rows[u], c), :] = o
        return st

    lax.fori_loop(0, seq // (c * GLA_UNROLL), chunk_state,
                  jnp.zeros((GLA_HEAD_DV, GLA_HEAD_DK), F32))


def _gla(l, proj3, ad3, wup, ba, ng):
    b, seq, _ = proj3.shape
    dk, dv = GLA_HEAD_DK, GLA_HEAD_DV
    return pl.pallas_call(
        _gla_kernel,
        grid=(b, GLA_HEADS),
        in_specs=[
            pl.BlockSpec((1, seq, dk), lambda i, h: (i, 0, OFF_GQ // dk + h)),
            pl.BlockSpec((1, seq, dk), lambda i, h: (i, 0, OFF_GK // dk + h)),
            pl.BlockSpec((1, seq, dv), lambda i, h: (i, 0, OFF_GV // dv + h)),
            pl.BlockSpec((1, seq, dv), lambda i, h: (i, 0, OFF_GG // dv + h)),
            pl.BlockSpec((1, seq, LANE), lambda i, h: (i, 0, 0)),
            _layer_spec(l, (LANE, dk), lambda i, h: (0, h)),
            _layer_spec(l, (1, dk), lambda i, h: (0, h)),
            _layer_spec(l, (1, dv), lambda i, h: (0, 0)),
        ],
        out_specs=pl.BlockSpec((1, seq, dv), lambda i, h: (i, 0, h)),
        out_shape=jax.ShapeDtypeStruct((b, seq, GLA_DV), BF16),
        scratch_shapes=[
            pltpu.VMEM((seq, dk), F32),
            pltpu.VMEM((seq, dk), BF16),
            pltpu.VMEM((seq, dv), F32),
            pltpu.VMEM((seq // GLA_CHUNK, dv, dk), F32),
            pltpu.VMEM((seq // GLA_CHUNK, SUBLANE, dk), F32),
        ],
        compiler_params=_params("parallel", "parallel"),
        name="gla",
    )(proj3, proj3, proj3, proj3, ad3, wup, ba, ng)


def _gelu_tanh(x):
    return 0.5 * x * (1.0 + jnp.tanh(0.7978845608028654 * (x + 0.044715 * (x * x * x))))


def _lru_kernel(x_ref, gate_ref, cw_ref, cb_ref, wa_ref, ba_ref, wx_ref, bx_ref, lam_ref, o_ref,
                a_ref, u_ref):
    seq = x_ref.shape[1]
    x = x_ref[0].astype(F32)
    nrow = SUBLANE
    sub = lax.broadcasted_iota(jnp.int32, (nrow, x.shape[1]), 0)

    def conv(xs, shifted):
        out = xs * cw_ref[CONV_WIDTH - 1:CONV_WIDTH, :] + cb_ref[...]
        for s in range(1, CONV_WIDTH):
            out = out + shifted(s) * cw_ref[CONV_WIDTH - 1 - s:CONV_WIDTH - s, :]
        return out

    xc = conv(x, lambda s: pltpu.roll(x, s, 0))
    head = x[0:nrow, :]
    xc_head = conv(head, lambda s: jnp.where(sub >= s, pltpu.roll(head, s, 0), 0.0))
    xc = jnp.concatenate([xc_head, xc[nrow:, :]], axis=0)

    xcb = xc.astype(BF16)
    neg_c_sp = (-LRU_C) * _softplus(-lam_ref[...])
    for blk in range(wa_ref.shape[0]):
        cs = slice(blk * LRU_BLOCK, (blk + 1) * LRU_BLOCK)
        xb = xcb[:, cs]
        r = _sigmoid(jnp.dot(xb, wa_ref[blk], preferred_element_type=F32) + ba_ref[:, cs])
        gi = _sigmoid(jnp.dot(xb, wx_ref[blk], preferred_element_type=F32) + bx_ref[:, cs])
        a = jnp.exp(r * neg_c_sp[:, cs])
        y = 1.0 - a * a
        root = jnp.where(y > 0.0, y * lax.rsqrt(y), 0.0)
        a_ref[:, cs] = a
        u_ref[:, cs] = root * gi * xc[:, cs]

    def body(gi, h):
        rows = [pl.multiple_of((gi * LRU_SCAN_UNROLL + j) * nrow, nrow)
                for j in range(LRU_SCAN_UNROLL)]
        tiles = [(a_ref[pl.ds(r0, nrow), :], u_ref[pl.ds(r0, nrow), :]) for r0 in rows]
        outs = []
        for av, uv in tiles:
            for s in (1, 2, 4):
                a_sh = jnp.where(sub >= s, pltpu.roll(av, s, 0), 1.0)
                u_sh = jnp.where(sub >= s, pltpu.roll(uv, s, 0), 0.0)
                uv = av * u_sh + uv
                av = av * a_sh
            hv = uv + av * h
            h = hv[nrow - 1:nrow, :]
            outs.append(hv)
        for r0, hv in zip(rows, outs):
            u_ref[pl.ds(r0, nrow), :] = hv
        return h

    lax.fori_loop(0, seq // (nrow * LRU_SCAN_UNROLL), body, jnp.zeros((1, x.shape[1]), F32))
    o_ref[0] = (u_ref[...] * _gelu_tanh(gate_ref[0].astype(F32))).astype(BF16)


def _lru(l, proj3, cw, cb, wa, ba, wx, bx, lam):
    b, seq, _ = proj3.shape
    nblk = LRU_BLOCKS_PER_STEP
    cblk = nblk * LRU_BLOCK
    vec = _layer_spec(l, (1, cblk), lambda i, h: (0, h))
    wspec = _layer_spec(l, (nblk, LRU_BLOCK, LRU_BLOCK), lambda i, h: (h, 0, 0))
    return pl.pallas_call(
        _lru_kernel,
        grid=(b, LRU_BLOCKS // nblk),
        in_specs=[
            pl.BlockSpec((1, seq, cblk), lambda i, h: (i, 0, OFF_LX // cblk + h)),
            pl.BlockSpec((1, seq, cblk), lambda i, h: (i, 0, OFF_LG // cblk + h)),
            _layer_spec(l, (CONV_WIDTH, cblk), lambda i, h: (0, h)),
            vec, wspec, vec, wspec, vec, vec,
        ],
        out_specs=pl.BlockSpec((1, seq, cblk), lambda i, h: (i, 0, h)),
        out_shape=jax.ShapeDtypeStruct((b, seq, LRU_WIDTH), BF16),
        scratch_shapes=[pltpu.VMEM((seq, cblk), F32), pltpu.VMEM((seq, cblk), F32)],
        compiler_params=_params("parallel", "parallel"),
        name="rglru",
    )(proj3, proj3, cw, cb, wa, ba, wx, bx, lam)


def _qk_norm(t, g):
    return t * lax.rsqrt(jnp.mean(t * t, axis=-1, keepdims=True) + EPS) * g


def _sb_kernel(q_ref, k_ref, v_ref, qg_ref, kg_ref, o_ref,
               kn_ref, qn_ref, acc_ref, car_ref, bias_ref):
    t, wk, d = SB_TILE, SB_WINDOW, SB_HEAD_DIM
    nt = SB_TILES_PER_STEP
    step = pl.program_id(1)
    row = lax.broadcasted_iota(jnp.int32, (t, wk), 0)
    col = lax.broadcasted_iota(jnp.int32, (t, wk), 1)
    col_minus_row = col - row
    tri = jnp.where(lax.broadcasted_iota(jnp.int32, (wk, wk), 0)
                    > lax.broadcasted_iota(jnp.int32, (wk, wk), 1), 1.0, 0.0).astype(BF16)

    def hs(h):
        return slice(h * d, (h + 1) * d)

    def rs(s):
        return slice(s * t, (s + 1) * t)

    @pl.when(step == 0)
    def _():
        for h in range(SB_HEADS):
            kn_ref[:, hs(h)] = _qk_norm(k_ref[0, :, hs(h)].astype(F32), kg_ref[...]).astype(BF16)

    chains = [(s, h) for s in range(nt) for h in range(SB_HEADS)]

    def window_pass(kstarts, first):
        worst = None
        for g0 in range(0, len(chains), SB_GROUP):
            group = chains[g0:g0 + SB_GROUP]
            if first:
                qs = [(_qk_norm(q_ref[0, rs(s), hs(h)].astype(F32), qg_ref[...])
                       * (d ** -0.5)).astype(BF16) for s, h in group]
                for (s, h), qh in zip(group, qs):
                    qn_ref[rs(s), hs(h)] = qh
            else:
                qs = [qn_ref[rs(s), hs(h)] for s, h in group]
            old = None if first else [(acc_ref[s * SB_HEADS + h], car_ref[s * SB_HEADS + h])
                                      for s, h in group]
            zs = [lax.dot_general(qh, kn_ref[pl.ds(kstarts[s], wk), hs(h)], NT_DIMS,
                                  preferred_element_type=F32) + bias_ref[s]
                  for qh, (s, h) in zip(qs, group)]
            lszs = [jnp.minimum(z, 0.0) - jnp.log(1.0 + jnp.exp(-jnp.abs(z))) for z in zs]
            lks = [lsz - z for lsz, z in zip(lszs, zs)]
            afters = [jnp.dot(lk.astype(BF16), tri, preferred_element_type=F32) for lk in lks]
            carries = [jnp.sum(lk, axis=-1, keepdims=True) for lk in lks]
            if not first:
                afters = [a + o[1] for a, o in zip(afters, old)]
                carries = [c + o[1] for c, o in zip(carries, old)]
            ws = [jnp.exp(lsz + after).astype(BF16) for lsz, after in zip(lszs, afters)]
            accs = [jnp.dot(w, v_ref[0, pl.ds(kstarts[s], wk), hs(h)], preferred_element_type=F32)
                    for w, (s, h) in zip(ws, group)]
            if not first:
                accs = [a + o[0] for a, o in zip(accs, old)]
            for (s, h), acc, carry in zip(group, accs, carries):
                acc_ref[s * SB_HEADS + h] = acc
                car_ref[s * SB_HEADS + h] = carry
                if first:
                    o_ref[0, rs(s), hs(h)] = acc.astype(BF16)
                worst = carry if worst is None else jnp.maximum(worst, carry)
        return worst

    def alive(carry):
        return (jnp.max(carry) > SB_DEAD).astype(jnp.int32)

    kstarts = []
    for s in range(nt):
        tile = step * nt + s
        kstart = pl.multiple_of(jnp.maximum(tile - 1, 0) * t, t)
        bias_ref[s] = jnp.where(col_minus_row < tile * t - kstart, 0.0, SB_MASKED)
        kstarts.append(kstart)
    worst = window_pass(kstarts, True)

    @pl.when(jnp.logical_and(alive(worst) > 0, step > 0))
    def _():
        def cond(st):
            return jnp.logical_and(st[nt - 1] > 0, st[nt] > 0)

        def body(st):
            kends = st[:nt]
            starts = [pl.multiple_of(jnp.maximum(kend - wk, 0), t) for kend in kends]
            for s in range(nt):
                bias_ref[s] = jnp.where(col < kends[s] - starts[s], 0.0, SB_MASKED)
            return tuple(starts) + (alive(window_pass(starts, False)),)

        lax.while_loop(cond, body, tuple(kstarts) + (jnp.int32(1),))
        for s, h in chains:
            o_ref[0, rs(s), hs(h)] = acc_ref[s * SB_HEADS + h].astype(BF16)


def _sb(l, proj3, qg, kg):
    b, seq, _ = proj3.shape
    rows = SB_TILES_PER_STEP * SB_TILE
    w = SB_HEADS * SB_HEAD_DIM
    nchain = SB_TILES_PER_STEP * SB_HEADS
    vec = _layer_spec(l, (1, SB_HEAD_DIM), lambda i, s: (0, 0))
    return pl.pallas_call(
        _sb_kernel,
        grid=(b, seq // rows),
        in_specs=[
            pl.BlockSpec((1, rows, w), lambda i, s: (i, s, OFF_SQ // w)),
            pl.BlockSpec((1, seq, w), lambda i, s: (i, 0, OFF_SK // w)),
            pl.BlockSpec((1, seq, w), lambda i, s: (i, 0, OFF_SV // w)),
            vec, vec,
        ],
        out_specs=pl.BlockSpec((1, rows, w), lambda i, s: (i, s, 0)),
        out_shape=jax.ShapeDtypeStruct((b, seq, w), BF16),
        scratch_shapes=[pltpu.VMEM((seq, w), BF16),
                        pltpu.VMEM((rows, w), BF16),
                        pltpu.VMEM((nchain, SB_TILE, SB_HEAD_DIM), F32),
                        pltpu.VMEM((nchain, SB_TILE, 1), F32),
                        pltpu.VMEM((SB_TILES_PER_STEP, SB_TILE, SB_WINDOW), F32)],
        compiler_params=_params("parallel", "arbitrary"),
        name="stickbreak",
    )(proj3, proj3, proj3, qg, kg)


def _merge_kernel(x_ref, oa_ref, ob_ref, oc_ref, ga_ref, gb_ref, gc_ref, bg_ref,
                  wa_ref, wb_ref, wc_ref, wo_ref, o_ref):
    def branch(o_r, w_r, gl_r, idx):
        gate = _sigmoid(gl_r[...].astype(F32) + bg_ref[:, idx * D_MODEL:(idx + 1) * D_MODEL])
        return gate * jnp.dot(o_r[...], w_r[...], preferred_element_type=F32)

    merged = (branch(oa_ref, wa_ref, ga_ref, 0) + branch(ob_ref, wb_ref, gb_ref, 1)
              + branch(oc_ref, wc_ref, gc_ref, 2))
    o_ref[...] = x_ref[...] + jnp.dot(merged.astype(BF16), wo_ref[...], preferred_element_type=F32)


def _merge(l, x2, oa, ob, oc, proj, bg, wa, wb, wc, wo, tm=512):
    n = x2.shape[0]
    rows = pl.BlockSpec((tm, D_MODEL), lambda i: (i, 0))
    wspec = _layer_spec(l, (D_MODEL, D_MODEL), lambda i: (0, 0))
    gate_blk = OFF_GATE // D_MODEL

    def gspec(idx):
        return pl.BlockSpec((tm, D_MODEL), lambda i: (i, gate_blk + idx))

    return pl.pallas_call(
        _merge_kernel,
        grid=(n // tm,),
        in_specs=[rows, rows, rows, rows, gspec(0), gspec(1), gspec(2),
                  _layer_spec(l, (1, 3 * D_MODEL), lambda i: (0, 0)),
                  wspec, wspec, wspec, wspec],
        out_specs=rows,
        out_shape=jax.ShapeDtypeStruct((n, D_MODEL), F32),
        compiler_params=_params("parallel"),
        name="merge_out",
    )(x2, oa, ob, oc, proj, proj, proj, bg, wa, wb, wc, wo)


def _mlp_kernel(x_ref, g_ref, wu_ref, wd_ref, o_ref, hn_ref):
    @pl.when(pl.program_id(1) == 0)
    def _():
        x = x_ref[...]
        hn = x * lax.rsqrt(jnp.mean(x * x, axis=-1, keepdims=True) + EPS) * g_ref[...]
        hn_ref[...] = hn.astype(BF16)
        o_ref[...] = x

    up = jnp.dot(hn_ref[...], wu_ref[...], preferred_element_type=F32)
    act = jnp.square(jnp.maximum(up, 0.0)).astype(BF16)
    o_ref[...] += jnp.dot(act, wd_ref[...], preferred_element_type=F32)


def _mlp(l, x2, g, wu, wd, tm=1024, tf=1024):
    n = x2.shape[0]
    return pl.pallas_call(
        _mlp_kernel,
        grid=(n // tm, D_FF // tf),
        in_specs=[
            pl.BlockSpec((tm, D_MODEL), lambda i, f: (i, 0)),
            _layer_spec(l, (1, D_MODEL), lambda i, f: (0, 0)),
            _layer_spec(l, (D_MODEL, tf), lambda i, f: (0, f)),
            _layer_spec(l, (tf, D_MODEL), lambda i, f: (f, 0)),
        ],
        out_specs=pl.BlockSpec((tm, D_MODEL), lambda i, f: (i, 0)),
        out_shape=jax.ShapeDtypeStruct((n, D_MODEL), F32),
        scratch_shapes=[pltpu.VMEM((tm, D_MODEL), BF16)],
        compiler_params=_params("parallel", "arbitrary"),
        name="mlp",
    )(x2, g, wu, wd)


def kernel(x, norm_mix_g, w_in, gla_w_up, gla_b_alpha, gla_norm_g, lru_conv_w, lru_conv_b, lru_w_a,
           lru_b_a, lru_w_x, lru_b_x, lru_lambda, sb_q_norm_g, sb_k_norm_g, w_branch_a, w_branch_b,
           w_branch_c, b_gate, w_out, norm_mlp_g, w_mlp_up, w_mlp_down):
    b, seq, d = x.shape
    depth = w_in.shape[0]
    ad0 = 2 * GLA_DK + 2 * GLA_DV
    w_main = jnp.concatenate([w_in[:, :, :ad0], w_in[:, :, ad0 + GLA_RANK:]], axis=-1).astype(BF16)
    w_ad = jnp.pad(w_in[:, :, ad0:ad0 + GLA_RANK], ((0, 0), (0, 0), (0, LANE - GLA_RANK))).astype(BF16)
    wup = jnp.pad(gla_w_up, ((0, 0), (0, LANE - GLA_RANK), (0, 0))).astype(BF16)
    wa_l, wx_l = lru_w_a.astype(BF16), lru_w_x.astype(BF16)
    wba, wbb, wbc = w_branch_a.astype(BF16), w_branch_b.astype(BF16), w_branch_c.astype(BF16)
    wo, wu, wd = w_out.astype(BF16), w_mlp_up.astype(BF16), w_mlp_down.astype(BF16)

    def vec(p):
        return p[:, None, :]

    mix_g, mlp_g = vec(norm_mix_g), vec(norm_mlp_g)
    b_alpha, gla_ng = vec(gla_b_alpha), vec(gla_norm_g)
    conv_b, b_a, b_x, lam = vec(lru_conv_b), vec(lru_b_a), vec(lru_b_x), vec(lru_lambda)
    qg, kg, bg = vec(sb_q_norm_g), vec(sb_k_norm_g), vec(b_gate)

    x2 = x.reshape(b * seq, d)
    for l in range(depth):
        proj, ad = _in_proj(l, x2, mix_g, w_main, w_ad)
        proj3 = proj.reshape(b, seq, PROJ_COLS)
        ad3 = ad.reshape(b, seq, LANE)
        o_a = _gla(l, proj3, ad3, wup, b_alpha, gla_ng)
        o_b = _lru(l, proj3, lru_conv_w, conv_b, wa_l, b_a, wx_l, b_x, lam)
        o_c = _sb(l, proj3, qg, kg)
        x2 = _merge(l, x2, o_a.reshape(b * seq, d), o_b.reshape(b * seq, d), o_c.reshape(b * seq, d),
                    proj, bg, wba, wbb, wbc, wo)
        x2 = _mlp(l, x2, mlp_g, wu, wd)
    return x2.reshape(b, seq, d)
```

```python
import jax
import jax.numpy as jnp
from jax import lax
from jax.experimental import pallas as pl
from jax.experimental.pallas import tpu as pltpu

F32 = jnp.float32
BF16 = jnp.bfloat16

D_MODEL = 1024
GLA_HEADS = 4
GLA_DK = 512
GLA_DV = 1024
GLA_HEAD_DK = GLA_DK // GLA_HEADS
GLA_HEAD_DV = GLA_DV // GLA_HEADS
GLA_RANK = 16
GLA_GATE_TAU = 16.0
GLA_CHUNK = 64
GLA_UNROLL = 8
LRU_WIDTH = 1024
LRU_BLOCKS = 4
LRU_BLOCK = LRU_WIDTH // LRU_BLOCKS
CONV_WIDTH = 4
LRU_C = 8.0
LRU_BLOCKS_PER_STEP = 2
LRU_SCAN_UNROLL = 8
SB_HEAD_DIM = 128
SB_HEADS = 8
SB_TILE = 128
SB_WINDOW = 256
SB_TILES_PER_STEP = 2
SB_GROUP = 8
SB_DEAD = -30.0
SB_MASKED = -1e30
D_FF = 4096
EPS = 1e-6

LANE = 128
SUBLANE = 8
VMEM_LIMIT = 48 * 1024 * 1024

OFF_GQ = 0
OFF_GK = OFF_GQ + GLA_DK
OFF_GV = OFF_GK + GLA_DK
OFF_GG = OFF_GV + GLA_DV
OFF_LX = OFF_GG + GLA_DV
OFF_LG = OFF_LX + LRU_WIDTH
OFF_SQ = OFF_LG + LRU_WIDTH
OFF_SK = OFF_SQ + D_MODEL
OFF_SV = OFF_SK + D_MODEL
OFF_GATE = OFF_SV + D_MODEL
PROJ_COLS = OFF_GATE + 3 * D_MODEL
TILE_LG, TILE_SQ, TILE_SK = OFF_LG // D_MODEL, OFF_SQ // D_MODEL, OFF_SK // D_MODEL

NT_DIMS = (((1,), (1,)), ((), ()))
TN_DIMS = (((0,), (0,)), ((), ()))


def _params(*sem):
    return pltpu.CompilerParams(dimension_semantics=sem, vmem_limit_bytes=VMEM_LIMIT)


def _layer_spec(l, shape, index_map):
    return pl.BlockSpec((None,) + shape, lambda *g: (l,) + index_map(*g))


def _softplus(z):
    return jnp.maximum(z, 0.0) + jnp.log(1.0 + jnp.exp(-jnp.abs(z)))


def _sigmoid(z):
    return 0.5 * jnp.tanh(0.5 * z) + 0.5


def _gelu_tanh(x):
    return 0.5 * x * (1.0 + jnp.tanh(0.7978845608028654 * (x + 0.044715 * (x * x * x))))


def _head_rms_norm(t, gain):
    heads = []
    for h in range(SB_HEADS):
        y = t[:, h * SB_HEAD_DIM:(h + 1) * SB_HEAD_DIM]
        heads.append(y * lax.rsqrt(jnp.mean(y * y, axis=-1, keepdims=True) + EPS))
    return jnp.concatenate(heads, axis=1) * gain


def _in_proj_kernel(x_ref, g_ref, w_ref, wad_ref, ep_ref, proj_ref, ad_ref, xn_ref):
    j = pl.program_id(1)

    @pl.when(j == 0)
    def _():
        x = x_ref[...]
        xn = x * lax.rsqrt(jnp.mean(x * x, axis=-1, keepdims=True) + EPS) * g_ref[...]
        xnb = xn.astype(BF16)
        xn_ref[...] = xnb
        ad_ref[...] = jnp.dot(xnb, wad_ref[...], preferred_element_type=F32)

    def tile(finish):
        acc = jnp.dot(xn_ref[...], w_ref[...], preferred_element_type=F32)
        proj_ref[...] = finish(acc).astype(BF16)

    is_gelu = j == TILE_LG
    is_qk = jnp.logical_or(j == TILE_SQ, j == TILE_SK)
    pl.when(jnp.logical_not(jnp.logical_or(is_gelu, is_qk)))(lambda: tile(lambda a: a))
    pl.when(is_gelu)(lambda: tile(_gelu_tanh))
    pl.when(is_qk)(lambda: tile(lambda a: _head_rms_norm(a, ep_ref[...])))


def _in_proj(l, x2, g, w_main, w_ad, ep, tm=2048, tn=D_MODEL):
    n = x2.shape[0]
    return pl.pallas_call(
        _in_proj_kernel,
        grid=(n // tm, PROJ_COLS // tn),
        in_specs=[
            pl.BlockSpec((tm, D_MODEL), lambda i, j: (i, 0)),
            _layer_spec(l, (1, D_MODEL), lambda i, j: (0, 0)),
            _layer_spec(l, (D_MODEL, tn), lambda i, j: (0, j)),
            _layer_spec(l, (D_MODEL, LANE), lambda i, j: (0, 0)),
            pl.BlockSpec((None, None, 1, tn), lambda i, j: (l, j, 0, 0)),
        ],
        out_specs=[
            pl.BlockSpec((tm, tn), lambda i, j: (i, j)),
            pl.BlockSpec((tm, LANE), lambda i, j: (i, 0)),
        ],
        out_shape=[
            jax.ShapeDtypeStruct((n, PROJ_COLS), BF16),
            jax.ShapeDtypeStruct((n, LANE), F32),
        ],
        scratch_shapes=[pltpu.VMEM((tm, D_MODEL), BF16)],
        compiler_params=_params("parallel", "arbitrary"),
        name="in_proj",
    )(x2, g, w_main, w_ad, ep)


def _gla_kernel(q_ref, k_ref, v_ref, g_ref, ad_ref, wup_ref, ba_ref, ng_ref, o_ref,
                la_ref, qd_ref, oi_ref, kv_ref, dec_ref):
    seq = q_ref.shape[1]
    c = GLA_CHUNK
    logits = jnp.dot(ad_ref[0].astype(BF16), wup_ref[...], preferred_element_type=F32) + ba_ref[...]
    la_ref[...] = -_softplus(-logits) * (1.0 / GLA_GATE_TAU)

    row = lax.broadcasted_iota(jnp.int32, (c, c), 0)
    col = lax.broadcasted_iota(jnp.int32, (c, c), 1)
    causal = row >= col
    tri = jnp.where(causal, 1.0, 0.0).astype(BF16)
    scale = GLA_HEAD_DK ** -0.5

    def chunk_local(gi, carry):
        rows = [pl.multiple_of((gi * GLA_UNROLL + u) * c, c) for u in range(GLA_UNROLL)]
        loaded = [(la_ref[pl.ds(r0, c), :], q_ref[0, pl.ds(r0, c), :], k_ref[0, pl.ds(r0, c), :],
                   v_ref[0, pl.ds(r0, c), :]) for r0 in rows]
        cums = [jnp.dot(tri, la.astype(BF16), preferred_element_type=F32)
                for la, _, _, _ in loaded]
        scaled = []
        for cum, (_, q, k, _) in zip(cums, loaded):
            last = cum[c - 1:c, :]
            k = k.astype(F32)
            scaled.append(((q.astype(F32) * scale * jnp.exp(cum)).astype(BF16),
                           (k * jnp.exp(-cum)).astype(BF16),
                           (k * jnp.exp(last - cum)).astype(BF16),
                           jnp.broadcast_to(jnp.exp(last), (SUBLANE, GLA_HEAD_DK))))
        scores = [lax.dot_general(q_dec, k_inv, NT_DIMS, preferred_element_type=F32)
                  for q_dec, k_inv, _, _ in scaled]
        kvs = [lax.dot_general(v, k_end, TN_DIMS, preferred_element_type=F32)
               for (_, _, k_end, _), (_, _, _, v) in zip(scaled, loaded)]
        scores = [jnp.where(causal, s, 0.0).astype(BF16) for s in scores]
        results = [(q_dec, jnp.dot(s, v, preferred_element_type=F32), kv, dec)
                   for s, kv, (q_dec, _, _, dec), (_, _, _, v) in zip(scores, kvs, scaled, loaded)]
        for u, (q_dec, o_intra, kv, dec) in enumerate(results):
            qd_ref[pl.ds(rows[u], c), :] = q_dec
            oi_ref[pl.ds(rows[u], c), :] = o_intra
            kv_ref[gi * GLA_UNROLL + u] = kv
            dec_ref[gi * GLA_UNROLL + u] = dec
        return carry

    lax.fori_loop(0, seq // (c * GLA_UNROLL), chunk_local, 0)

    def chunk_state(gi, st):
        rows = [pl.multiple_of((gi * GLA_UNROLL + u) * c, c) for u in range(GLA_UNROLL)]
        loaded = [(oi_ref[pl.ds(r0, c), :], qd_ref[pl.ds(r0, c), :], g_ref[0, pl.ds(r0, c), :],
                   kv_ref[gi * GLA_UNROLL + u], dec_ref[gi * GLA_UNROLL + u])
                  for u, r0 in enumerate(rows)]
        outs = []
        for o_intra, q_dec, g, kv, dec in loaded:
            o = o_intra + lax.dot_general(q_dec, st.astype(BF16), NT_DIMS,
                                          preferred_element_type=F32)
            st = st * dec[0:1, :] + kv
            o = o * lax.rsqrt(jnp.mean(o * o, axis=-1, keepdims=True) + EPS) * ng_ref[...]
            g = g.astype(F32)
            outs.append((o * (g * _sigmoid(g))).astype(BF16))
        for u, o in enumerate(outs):
            o_ref[0, pl.ds(rows[u], c), :] = o
        return st

    lax.fori_loop(0, seq // (c * GLA_UNROLL), chunk_state,
                  jnp.zeros((GLA_HEAD_DV, GLA_HEAD_DK), F32))


def _gla(l, proj3, ad3, wup, ba, ng):
    b, seq, _ = proj3.shape
    dk, dv = GLA_HEAD_DK, GLA_HEAD_DV
    return pl.pallas_call(
        _gla_kernel,
        grid=(b, GLA_HEADS),
        in_specs=[
            pl.BlockSpec((1, seq, dk), lambda i, h: (i, 0, OFF_GQ // dk + h)),
            pl.BlockSpec((1, seq, dk), lambda i, h: (i, 0, OFF_GK // dk + h)),
            pl.BlockSpec((1, seq, dv), lambda i, h: (i, 0, OFF_GV // dv + h)),
            pl.BlockSpec((1, seq, dv), lambda i, h: (i, 0, OFF_GG // dv + h)),
            pl.BlockSpec((1, seq, LANE), lambda i, h: (i, 0, 0)),
            _layer_spec(l, (LANE, dk), lambda i, h: (0, h)),
            _layer_spec(l, (1, dk), lambda i, h: (0, h)),
            _layer_spec(l, (1, dv), lambda i, h: (0, 0)),
        ],
        out_specs=pl.BlockSpec((1, seq, dv), lambda i, h: (i, 0, h)),
        out_shape=jax.ShapeDtypeStruct((b, seq, GLA_DV), BF16),
        scratch_shapes=[
            pltpu.VMEM((seq, dk), F32),
            pltpu.VMEM((seq, dk), BF16),
            pltpu.VMEM((seq, dv), F32),
            pltpu.VMEM((seq // GLA_CHUNK, dv, dk), F32),
            pltpu.VMEM((seq // GLA_CHUNK, SUBLANE, dk), F32),
        ],
        compiler_params=_params("parallel", "parallel"),
        name="gla",
    )(proj3, proj3, proj3, proj3, ad3, wup, ba, ng)


def _lru_kernel(x_ref, gate_ref, cw_ref, cb_ref, wa_ref, ba_ref, wx_ref, bx_ref, lam_ref, o_ref,
                a_ref, u_ref):
    seq = x_ref.shape[1]
    x = x_ref[0].astype(F32)
    nrow = SUBLANE
    sub = lax.broadcasted_iota(jnp.int32, (nrow, x.shape[1]), 0)

    def conv(xs, shifted):
        out = xs * cw_ref[CONV_WIDTH - 1:CONV_WIDTH, :] + cb_ref[...]
        for s in range(1, CONV_WIDTH):
            out = out + shifted(s) * cw_ref[CONV_WIDTH - 1 - s:CONV_WIDTH - s, :]
        return out

    xc = conv(x, lambda s: pltpu.roll(x, s, 0))
    head = x[0:nrow, :]
    xc_head = conv(head, lambda s: jnp.where(sub >= s, pltpu.roll(head, s, 0), 0.0))
    xc = jnp.concatenate([xc_head, xc[nrow:, :]], axis=0)

    xcb = xc.astype(BF16)
    neg_c_sp = (-LRU_C) * _softplus(-lam_ref[...])
    for blk in range(wa_ref.shape[0]):
        cs = slice(blk * LRU_BLOCK, (blk + 1) * LRU_BLOCK)
        xb = xcb[:, cs]
        r = _sigmoid(jnp.dot(xb, wa_ref[blk], preferred_element_type=F32) + ba_ref[:, cs])
        gi = _sigmoid(jnp.dot(xb, wx_ref[blk], preferred_element_type=F32) + bx_ref[:, cs])
        a = jnp.exp(r * neg_c_sp[:, cs])
        y = 1.0 - a * a
        root = jnp.where(y > 0.0, y * lax.rsqrt(y), 0.0)
        a_ref[:, cs] = a
        u_ref[:, cs] = root * gi * xc[:, cs]

    def body(gi, h):
        rows = [pl.multiple_of((gi * LRU_SCAN_UNROLL + j) * nrow, nrow)
                for j in range(LRU_SCAN_UNROLL)]
        tiles = [(a_ref[pl.ds(r0, nrow), :], u_ref[pl.ds(r0, nrow), :]) for r0 in rows]
        outs = []
        for av, uv in tiles:
            for s in (1, 2, 4):
                a_sh = jnp.where(sub >= s, pltpu.roll(av, s, 0), 1.0)
                u_sh = jnp.where(sub >= s, pltpu.roll(uv, s, 0), 0.0)
                uv = av * u_sh + uv
                av = av * a_sh
            hv = uv + av * h
            h = hv[nrow - 1:nrow, :]
            outs.append(hv)
        for r0, hv in zip(rows, outs):
            u_ref[pl.ds(r0, nrow), :] = hv
        return h

    lax.fori_loop(0, seq // (nrow * LRU_SCAN_UNROLL), body, jnp.zeros((1, x.shape[1]), F32))
    o_ref[0] = (u_ref[...] * gate_ref[0].astype(F32)).astype(BF16)


def _lru(l, proj3, cw, cb, wa, ba, wx, bx, lam):
    b, seq, _ = proj3.shape
    nblk = LRU_BLOCKS_PER_STEP
    cblk = nblk * LRU_BLOCK
    vec = _layer_spec(l, (1, cblk), lambda i, h: (0, h))
    wspec = _layer_spec(l, (nblk, LRU_BLOCK, LRU_BLOCK), lambda i, h: (h, 0, 0))
    return pl.pallas_call(
        _lru_kernel,
        grid=(b, LRU_BLOCKS // nblk),
        in_specs=[
            pl.BlockSpec((1, seq, cblk), lambda i, h: (i, 0, OFF_LX // cblk + h)),
            pl.BlockSpec((1, seq, cblk), lambda i, h: (i, 0, OFF_LG // cblk + h)),
            _layer_spec(l, (CONV_WIDTH, cblk), lambda i, h: (0, h)),
            vec, wspec, vec, wspec, vec, vec,
        ],
        out_specs=pl.BlockSpec((1, seq, cblk), lambda i, h: (i, 0, h)),
        out_shape=jax.ShapeDtypeStruct((b, seq, LRU_WIDTH), BF16),
        scratch_shapes=[pltpu.VMEM((seq, cblk), F32), pltpu.VMEM((seq, cblk), F32)],
        compiler_params=_params("parallel", "parallel"),
        name="rglru",
    )(proj3, proj3, cw, cb, wa, ba, wx, bx, lam)


def _sb_kernel(q_ref, k_ref, v_ref, o_ref, acc_ref, car_ref, bias_ref):
    t, wk, d = SB_TILE, SB_WINDOW, SB_HEAD_DIM
    nt = SB_TILES_PER_STEP
    step = pl.program_id(1)
    row = lax.broadcasted_iota(jnp.int32, (t, wk), 0)
    col = lax.broadcasted_iota(jnp.int32, (t, wk), 1)
    col_minus_row = col - row
    tri = jnp.where(lax.broadcasted_iota(jnp.int32, (wk, wk), 0)
                    > lax.broadcasted_iota(jnp.int32, (wk, wk), 1), 1.0, 0.0).astype(BF16)

    def hs(h):
        return slice(h * d, (h + 1) * d)

    def rs(s):
        return slice(s * t, (s + 1) * t)

    chains = [(s, h) for s in range(nt) for h in range(SB_HEADS)]

    def window_pass(kstarts, first):
        worst = None
        for g0 in range(0, len(chains), SB_GROUP):
            group = chains[g0:g0 + SB_GROUP]
            old = None if first else [(acc_ref[s * SB_HEADS + h], car_ref[s * SB_HEADS + h])
                                      for s, h in group]
            zs = [lax.dot_general(q_ref[0, rs(s), hs(h)], k_ref[0, pl.ds(kstarts[s], wk), hs(h)],
                                  NT_DIMS, preferred_element_type=F32) + bias_ref[s]
                  for s, h in group]
            lszs = [jnp.minimum(z, 0.0) - jnp.log(1.0 + jnp.exp(-jnp.abs(z))) for z in zs]
            lks = [lsz - z for lsz, z in zip(lszs, zs)]
            afters = [jnp.dot(lk.astype(BF16), tri, preferred_element_type=F32) for lk in lks]
            carries = [jnp.sum(lk, axis=-1, keepdims=True) for lk in lks]
            if not first:
                afters = [a + o[1] for a, o in zip(afters, old)]
                carries = [c + o[1] for c, o in zip(carries, old)]
            ws = [jnp.exp(lsz + after).astype(BF16) for lsz, after in zip(lszs, afters)]
            accs = [jnp.dot(w, v_ref[0, pl.ds(kstarts[s], wk), hs(h)], preferred_element_type=F32)
                    for w, (s, h) in zip(ws, group)]
            if not first:
                accs = [a + o[0] for a, o in zip(accs, old)]
            for (s, h), acc, carry in zip(group, accs, carries):
                acc_ref[s * SB_HEADS + h] = acc
                car_ref[s * SB_HEADS + h] = carry
                if first:
                    o_ref[0, rs(s), hs(h)] = acc.astype(BF16)
                worst = carry if worst is None else jnp.maximum(worst, carry)
        return worst

    def alive(carry):
        return (jnp.max(carry) > SB_DEAD).astype(jnp.int32)

    kstarts = []
    for s in range(nt):
        tile = step * nt + s
        kstart = pl.multiple_of(jnp.maximum(tile - 1, 0) * t, t)
        bias_ref[s] = jnp.where(col_minus_row < tile * t - kstart, 0.0, SB_MASKED)
        kstarts.append(kstart)
    worst = window_pass(kstarts, True)

    @pl.when(jnp.logical_and(alive(worst) > 0, step > 0))
    def _():
        def cond(st):
            return jnp.logical_and(st[nt - 1] > 0, st[nt] > 0)

        def body(st):
            kends = st[:nt]
            starts = [pl.multiple_of(jnp.maximum(kend - wk, 0), t) for kend in kends]
            for s in range(nt):
                bias_ref[s] = jnp.where(col < kends[s] - starts[s], 0.0, SB_MASKED)
            return tuple(starts) + (alive(window_pass(starts, False)),)

        lax.while_loop(cond, body, tuple(kstarts) + (jnp.int32(1),))
        for s, h in chains:
            o_ref[0, rs(s), hs(h)] = acc_ref[s * SB_HEADS + h].astype(BF16)


def _sb(proj3):
    b, seq, _ = proj3.shape
    rows = SB_TILES_PER_STEP * SB_TILE
    w = SB_HEADS * SB_HEAD_DIM
    nchain = SB_TILES_PER_STEP * SB_HEADS
    return pl.pallas_call(
        _sb_kernel,
        grid=(b, seq // rows),
        in_specs=[
            pl.BlockSpec((1, rows, w), lambda i, s: (i, s, OFF_SQ // w)),
            pl.BlockSpec((1, seq, w), lambda i, s: (i, 0, OFF_SK // w)),
            pl.BlockSpec((1, seq, w), lambda i, s: (i, 0, OFF_SV // w)),
        ],
        out_specs=pl.BlockSpec((1, rows, w), lambda i, s: (i, s, 0)),
        out_shape=jax.ShapeDtypeStruct((b, seq, w), BF16),
        scratch_shapes=[pltpu.VMEM((nchain, SB_TILE, SB_HEAD_DIM), F32),
                        pltpu.VMEM((nchain, SB_TILE, 1), F32),
                        pltpu.VMEM((SB_TILES_PER_STEP, SB_TILE, SB_WINDOW), F32)],
        compiler_params=_params("parallel", "arbitrary"),
        name="stickbreak",
    )(proj3, proj3, proj3)


def _merge_kernel(x_ref, oa_ref, ob_ref, oc_ref, ga_ref, gb_ref, gc_ref, bg_ref,
                  wa_ref, wb_ref, wc_ref, wo_ref, o_ref):
    def branch(o_r, w_r, gl_r, idx):
        gate = _sigmoid(gl_r[...].astype(F32) + bg_ref[:, idx * D_MODEL:(idx + 1) * D_MODEL])
        return gate * jnp.dot(o_r[...], w_r[...], preferred_element_type=F32)

    merged = (branch(oa_ref, wa_ref, ga_ref, 0) + branch(ob_ref, wb_ref, gb_ref, 1)
              + branch(oc_ref, wc_ref, gc_ref, 2))
    o_ref[...] = x_ref[...] + jnp.dot(merged.astype(BF16), wo_ref[...], preferred_element_type=F32)


def _merge(l, x2, oa, ob, oc, proj, bg, wa, wb, wc, wo, tm=512):
    n = x2.shape[0]
    rows = pl.BlockSpec((tm, D_MODEL), lambda i: (i, 0))
    wspec = _layer_spec(l, (D_MODEL, D_MODEL), lambda i: (0, 0))
    gate_blk = OFF_GATE // D_MODEL

    def gspec(idx):
        return pl.BlockSpec((tm, D_MODEL), lambda i: (i, gate_blk + idx))

    return pl.pallas_call(
        _merge_kernel,
        grid=(n // tm,),
        in_specs=[rows, rows, rows, rows, gspec(0), gspec(1), gspec(2),
                  _layer_spec(l, (1, 3 * D_MODEL), lambda i: (0, 0)),
                  wspec, wspec, wspec, wspec],
        out_specs=rows,
        out_shape=jax.ShapeDtypeStruct((n, D_MODEL), F32),
        compiler_params=_params("parallel"),
        name="merge_out",
    )(x2, oa, ob, oc, proj, proj, proj, bg, wa, wb, wc, wo)


def _mlp_kernel(x_ref, g_ref, wu_ref, wd_ref, o_ref, hn_ref):
    @pl.when(pl.program_id(1) == 0)
    def _():
        x = x_ref[...]
        hn = x * lax.rsqrt(jnp.mean(x * x, axis=-1, keepdims=True) + EPS) * g_ref[...]
        hn_ref[...] = hn.astype(BF16)
        o_ref[...] = x

    up = jnp.dot(hn_ref[...], wu_ref[...], preferred_element_type=F32)
    act = jnp.square(jnp.maximum(up, 0.0)).astype(BF16)
    o_ref[...] += jnp.dot(act, wd_ref[...], preferred_element_type=F32)


def _mlp(l, x2, g, wu, wd, tm=1024, tf=1024):
    n = x2.shape[0]
    return pl.pallas_call(
        _mlp_kernel,
        grid=(n // tm, D_FF // tf),
        in_specs=[
            pl.BlockSpec((tm, D_MODEL), lambda i, f: (i, 0)),
            _layer_spec(l, (1, D_MODEL), lambda i, f: (0, 0)),
            _layer_spec(l, (D_MODEL, tf), lambda i, f: (0, f)),
            _layer_spec(l, (tf, D_MODEL), lambda i, f: (f, 0)),
        ],
        out_specs=pl.BlockSpec((tm, D_MODEL), lambda i, f: (i, 0)),
        out_shape=jax.ShapeDtypeStruct((n, D_MODEL), F32),
        scratch_shapes=[pltpu.VMEM((tm, D_MODEL), BF16)],
        compiler_params=_params("parallel", "arbitrary"),
        name="mlp",
    )(x2, g, wu, wd)


def kernel(x, norm_mix_g, w_in, gla_w_up, gla_b_alpha, gla_norm_g, lru_conv_w, lru_conv_b, lru_w_a,
           lru_b_a, lru_w_x, lru_b_x, lru_lambda, sb_q_norm_g, sb_k_norm_g, w_branch_a, w_branch_b,
           w_branch_c, b_gate, w_out, norm_mlp_g, w_mlp_up, w_mlp_down):
    b, seq, d = x.shape
    depth = w_in.shape[0]
    ad0 = 2 * GLA_DK + 2 * GLA_DV
    w_in = w_in.astype(BF16)
    w_main = jnp.concatenate([w_in[:, :, :ad0], w_in[:, :, ad0 + GLA_RANK:]], axis=-1)
    w_ad = jnp.pad(w_in[:, :, ad0:ad0 + GLA_RANK], ((0, 0), (0, 0), (0, LANE - GLA_RANK)))
    wup = jnp.pad(gla_w_up, ((0, 0), (0, LANE - GLA_RANK), (0, 0))).astype(BF16)
    wa_l, wx_l = lru_w_a.astype(BF16), lru_w_x.astype(BF16)
    wba, wbb, wbc = w_branch_a.astype(BF16), w_branch_b.astype(BF16), w_branch_c.astype(BF16)
    wo, wu, wd = w_out.astype(BF16), w_mlp_up.astype(BF16), w_mlp_down.astype(BF16)

    def vec(p):
        return p[:, None, :]

    mix_g, mlp_g = vec(norm_mix_g), vec(norm_mlp_g)
    b_alpha, gla_ng = vec(gla_b_alpha), vec(gla_norm_g)
    conv_b, b_a, b_x, lam = vec(lru_conv_b), vec(lru_b_a), vec(lru_b_x), vec(lru_lambda)

    ep = jnp.zeros((depth, PROJ_COLS // D_MODEL, D_MODEL), F32)
    ep = ep.at[:, TILE_SQ].set(jnp.tile(sb_q_norm_g * SB_HEAD_DIM ** -0.5, (1, SB_HEADS)))
    ep = ep.at[:, TILE_SK].set(jnp.tile(sb_k_norm_g, (1, SB_HEADS)))
    ep = ep[:, :, None, :]
    bg = vec(b_gate)

    x2 = x.reshape(b * seq, d)
    for l in range(depth):
        proj, ad = _in_proj(l, x2, mix_g, w_main, w_ad, ep)
        proj3 = proj.reshape(b, seq, PROJ_COLS)
        ad3 = ad.reshape(b, seq, LANE)
        o_a = _gla(l, proj3, ad3, wup, b_alpha, gla_ng)
        o_b = _lru(l, proj3, lru_conv_w, conv_b, wa_l, b_a, wx_l, b_x, lam)
        o_c = _sb(proj3)
        x2 = _merge(l, x2, o_a.reshape(b * seq, d), o_b.reshape(b * seq, d), o_c.reshape(b * seq, d),
                    proj, bg, wba, wbb, wbc, wo)
        x2 = _mlp(l, x2, mlp_g, wu, wd)
    return x2.reshape(b, seq, d)
```

```python
import jax
import jax.numpy as jnp
from jax import lax
from jax.experimental import pallas as pl
from jax.experimental.pallas import tpu as pltpu

F32 = jnp.float32
BF16 = jnp.bfloat16

D_MODEL = 1024
GLA_HEADS = 4
GLA_DK = 512
GLA_DV = 1024
GLA_HEAD_DK = GLA_DK // GLA_HEADS
GLA_HEAD_DV = GLA_DV // GLA_HEADS
GLA_RANK = 16
GLA_GATE_TAU = 16.0
GLA_CHUNK = 64
GLA_UNROLL = 16
LRU_WIDTH = 1024
LRU_BLOCKS = 4
LRU_BLOCK = LRU_WIDTH // LRU_BLOCKS
CONV_WIDTH = 4
LRU_C = 8.0
LRU_BLOCKS_PER_STEP = 2
LRU_SCAN_UNROLL = 8
SB_HEAD_DIM = 128
SB_HEADS = 8
SB_TILE = 128
SB_WINDOW = 256
SB_TILES_PER_STEP = 4
SB_GROUP = 8
SB_DEAD = -30.0
SB_MASKED = -1e30
D_FF = 4096
EPS = 1e-6

LANE = 128
SUBLANE = 8
VMEM_LIMIT = 48 * 1024 * 1024

OFF_GQ = 0
OFF_GK = OFF_GQ + GLA_DK
OFF_GV = OFF_GK + GLA_DK
OFF_GG = OFF_GV + GLA_DV
OFF_LX = OFF_GG + GLA_DV
OFF_LG = OFF_LX + LRU_WIDTH
OFF_SQ = OFF_LG + LRU_WIDTH
OFF_SK = OFF_SQ + D_MODEL
OFF_SV = OFF_SK + D_MODEL
OFF_GATE = OFF_SV + D_MODEL
PROJ_COLS = OFF_GATE + 3 * D_MODEL
TILE_LG, TILE_SQ, TILE_SK = OFF_LG // D_MODEL, OFF_SQ // D_MODEL, OFF_SK // D_MODEL

NT_DIMS = (((1,), (1,)), ((), ()))
TN_DIMS = (((0,), (0,)), ((), ()))


def _params(*sem):
    return pltpu.CompilerParams(dimension_semantics=sem, vmem_limit_bytes=VMEM_LIMIT)


def _layer_spec(l, shape, index_map):
    return pl.BlockSpec((None,) + shape, lambda *g: (l,) + index_map(*g))


def _softplus(z):
    return jnp.maximum(z, 0.0) + jnp.log(1.0 + jnp.exp(-jnp.abs(z)))


def _sigmoid(z):
    return 0.5 * jnp.tanh(0.5 * z) + 0.5


def _gelu_tanh(x):
    return 0.5 * x * (1.0 + jnp.tanh(0.7978845608028654 * (x + 0.044715 * (x * x * x))))


def _head_rms_norm(t, gain):
    heads = []
    for h in range(SB_HEADS):
        y = t[:, h * SB_HEAD_DIM:(h + 1) * SB_HEAD_DIM]
        heads.append(y * lax.rsqrt(jnp.mean(y * y, axis=-1, keepdims=True) + EPS))
    return jnp.concatenate(heads, axis=1) * gain


def _in_proj_kernel(x_ref, g_ref, w_ref, wad_ref, ep_ref, proj_ref, ad_ref, xn_ref):
    j = pl.program_id(1)

    @pl.when(j == 0)
    def _():
        x = x_ref[...]
        xn = x * lax.rsqrt(jnp.mean(x * x, axis=-1, keepdims=True) + EPS) * g_ref[...]
        xnb = xn.astype(BF16)
        xn_ref[...] = xnb
        ad_ref[...] = jnp.dot(xnb, wad_ref[...], preferred_element_type=F32)

    def tile(finish):
        acc = jnp.dot(xn_ref[...], w_ref[...], preferred_element_type=F32)
        proj_ref[...] = finish(acc).astype(BF16)

    is_gelu = j == TILE_LG
    is_qk = jnp.logical_or(j == TILE_SQ, j == TILE_SK)
    pl.when(jnp.logical_not(jnp.logical_or(is_gelu, is_qk)))(lambda: tile(lambda a: a))
    pl.when(is_gelu)(lambda: tile(_gelu_tanh))
    pl.when(is_qk)(lambda: tile(lambda a: _head_rms_norm(a, ep_ref[...])))


def _in_proj(l, x2, g, w_main, w_ad, ep, tm=2048, tn=D_MODEL):
    n = x2.shape[0]
    return pl.pallas_call(
        _in_proj_kernel,
        grid=(n // tm, PROJ_COLS // tn),
        in_specs=[
            pl.BlockSpec((tm, D_MODEL), lambda i, j: (i, 0)),
            _layer_spec(l, (1, D_MODEL), lambda i, j: (0, 0)),
            _layer_spec(l, (D_MODEL, tn), lambda i, j: (0, j)),
            _layer_spec(l, (D_MODEL, LANE), lambda i, j: (0, 0)),
            pl.BlockSpec((None, None, 1, tn), lambda i, j: (l, j, 0, 0)),
        ],
        out_specs=[
            pl.BlockSpec((tm, tn), lambda i, j: (i, j)),
            pl.BlockSpec((tm, LANE), lambda i, j: (i, 0)),
        ],
        out_shape=[
            jax.ShapeDtypeStruct((n, PROJ_COLS), BF16),
            jax.ShapeDtypeStruct((n, LANE), F32),
        ],
        scratch_shapes=[pltpu.VMEM((tm, D_MODEL), BF16)],
        compiler_params=_params("parallel", "arbitrary"),
        name="in_proj",
    )(x2, g, w_main, w_ad, ep)


def _gla_kernel(q_ref, k_ref, v_ref, g_ref, ad_ref, wup_ref, ba_ref, ng_ref, o_ref,
                la_ref, qd_ref, oi_ref, kv_ref, dec_ref):
    seq = q_ref.shape[1]
    c = GLA_CHUNK
    logits = jnp.dot(ad_ref[0].astype(BF16), wup_ref[...], preferred_element_type=F32) + ba_ref[...]
    la_ref[...] = -_softplus(-logits) * (1.0 / GLA_GATE_TAU)

    row = lax.broadcasted_iota(jnp.int32, (c, c), 0)
    col = lax.broadcasted_iota(jnp.int32, (c, c), 1)
    causal = row >= col
    tri = jnp.where(causal, 1.0, 0.0).astype(BF16)
    scale = GLA_HEAD_DK ** -0.5

    def chunk_local(gi, carry):
        rows = [pl.multiple_of((gi * GLA_UNROLL + u) * c, c) for u in range(GLA_UNROLL)]
        loaded = [(la_ref[pl.ds(r0, c), :], q_ref[0, pl.ds(r0, c), :], k_ref[0, pl.ds(r0, c), :],
                   v_ref[0, pl.ds(r0, c), :]) for r0 in rows]
        cums = [jnp.dot(tri, la.astype(BF16), preferred_element_type=F32)
                for la, _, _, _ in loaded]
        scaled = []
        for cum, (_, q, k, _) in zip(cums, loaded):
            last = cum[c - 1:c, :]
            k = k.astype(F32)
            scaled.append(((q.astype(F32) * scale * jnp.exp(cum)).astype(BF16),
                           (k * jnp.exp(-cum)).astype(BF16),
                           (k * jnp.exp(last - cum)).astype(BF16),
                           jnp.broadcast_to(jnp.exp(last), (SUBLANE, GLA_HEAD_DK))))
        scores = [lax.dot_general(q_dec, k_inv, NT_DIMS, preferred_element_type=F32)
                  for q_dec, k_inv, _, _ in scaled]
        kvs = [lax.dot_general(v, k_end, TN_DIMS, preferred_element_type=F32)
               for (_, _, k_end, _), (_, _, _, v) in zip(scaled, loaded)]
        scores = [jnp.where(causal, s, 0.0).astype(BF16) for s in scores]
        results = [(q_dec, jnp.dot(s, v, preferred_element_type=F32), kv, dec)
                   for s, kv, (q_dec, _, _, dec), (_, _, _, v) in zip(scores, kvs, scaled, loaded)]
        for u, (q_dec, o_intra, kv, dec) in enumerate(results):
            qd_ref[pl.ds(rows[u], c), :] = q_dec
            oi_ref[pl.ds(rows[u], c), :] = o_intra
            kv_ref[gi * GLA_UNROLL + u] = kv
            dec_ref[gi * GLA_UNROLL + u] = dec
        return carry

    lax.fori_loop(0, seq // (c * GLA_UNROLL), chunk_local, 0)

    def chunk_state(gi, st):
        rows = [pl.multiple_of((gi * GLA_UNROLL + u) * c, c) for u in range(GLA_UNROLL)]
        loaded = [(oi_ref[pl.ds(r0, c), :], qd_ref[pl.ds(r0, c), :], g_ref[0, pl.ds(r0, c), :],
                   kv_ref[gi * GLA_UNROLL + u], dec_ref[gi * GLA_UNROLL + u])
                  for u, r0 in enumerate(rows)]
        outs = []
        for o_intra, q_dec, g, kv, dec in loaded:
            o = o_intra + lax.dot_general(q_dec, st.astype(BF16), NT_DIMS,
                                          preferred_element_type=F32)
            st = st * dec[0:1, :] + kv
            o = o * lax.rsqrt(jnp.mean(o * o, axis=-1, keepdims=True) + EPS) * ng_ref[...]
            g = g.astype(F32)
            outs.append((o * (g * _sigmoid(g))).astype(BF16))
        for u, o in enumerate(outs):
            o_ref[0, pl.ds(rows[u], c), :] = o
        return st

    lax.fori_loop(0, seq // (c * GLA_UNROLL), chunk_state,
                  jnp.zeros((GLA_HEAD_DV, GLA_HEAD_DK), F32))


def _gla(l, proj3, ad3, wup, ba, ng):
    b, seq, _ = proj3.shape
    dk, dv = GLA_HEAD_DK, GLA_HEAD_DV
    return pl.pallas_call(
        _gla_kernel,
        grid=(b, GLA_HEADS),
        in_specs=[
            pl.BlockSpec((1, seq, dk), lambda i, h: (i, 0, OFF_GQ // dk + h)),
            pl.BlockSpec((1, seq, dk), lambda i, h: (i, 0, OFF_GK // dk + h)),
            pl.BlockSpec((1, seq, dv), lambda i, h: (i, 0, OFF_GV // dv + h)),
            pl.BlockSpec((1, seq, dv), lambda i, h: (i, 0, OFF_GG // dv + h)),
            pl.BlockSpec((1, seq, LANE), lambda i, h: (i, 0, 0)),
            _layer_spec(l, (LANE, dk), lambda i, h: (0, h)),
            _layer_spec(l, (1, dk), lambda i, h: (0, h)),
            _layer_spec(l, (1, dv), lambda i, h: (0, 0)),
        ],
        out_specs=pl.BlockSpec((1, seq, dv), lambda i, h: (i, 0, h)),
        out_shape=jax.ShapeDtypeStruct((b, seq, GLA_DV), BF16),
        scratch_shapes=[
            pltpu.VMEM((seq, dk), F32),
            pltpu.VMEM((seq, dk), BF16),
            pltpu.VMEM((seq, dv), F32),
            pltpu.VMEM((seq // GLA_CHUNK, dv, dk), F32),
            pltpu.VMEM((seq // GLA_CHUNK, SUBLANE, dk), F32),
        ],
        compiler_params=_params("parallel", "parallel"),
        name="gla",
    )(proj3, proj3, proj3, proj3, ad3, wup, ba, ng)


def _lru_kernel(x_ref, gate_ref, cw_ref, cb_ref, wa_ref, ba_ref, wx_ref, bx_ref, lam_ref, o_ref,
                a_ref, u_ref):
    seq = x_ref.shape[1]
    x = x_ref[0].astype(F32)
    nrow = SUBLANE
    sub = lax.broadcasted_iota(jnp.int32, (nrow, x.shape[1]), 0)

    def conv(xs, shifted):
        out = xs * cw_ref[CONV_WIDTH - 1:CONV_WIDTH, :] + cb_ref[...]
        for s in range(1, CONV_WIDTH):
            out = out + shifted(s) * cw_ref[CONV_WIDTH - 1 - s:CONV_WIDTH - s, :]
        return out

    xc = conv(x, lambda s: pltpu.roll(x, s, 0))
    head = x[0:nrow, :]
    xc_head = conv(head, lambda s: jnp.where(sub >= s, pltpu.roll(head, s, 0), 0.0))
    xc = jnp.concatenate([xc_head, xc[nrow:, :]], axis=0)

    xcb = xc.astype(BF16)
    neg_c_sp = (-LRU_C) * _softplus(-lam_ref[...])
    for blk in range(wa_ref.shape[0]):
        cs = slice(blk * LRU_BLOCK, (blk + 1) * LRU_BLOCK)
        xb = xcb[:, cs]
        r = _sigmoid(jnp.dot(xb, wa_ref[blk], preferred_element_type=F32) + ba_ref[:, cs])
        gi = _sigmoid(jnp.dot(xb, wx_ref[blk], preferred_element_type=F32) + bx_ref[:, cs])
        a = jnp.exp(r * neg_c_sp[:, cs])
        y = 1.0 - a * a
        root = jnp.where(y > 0.0, y * lax.rsqrt(y), 0.0)
        a_ref[:, cs] = a
        u_ref[:, cs] = root * gi * xc[:, cs]

    def body(gi, h):
        rows = [pl.multiple_of((gi * LRU_SCAN_UNROLL + j) * nrow, nrow)
                for j in range(LRU_SCAN_UNROLL)]
        tiles = [(a_ref[pl.ds(r0, nrow), :], u_ref[pl.ds(r0, nrow), :]) for r0 in rows]
        outs = []
        for av, uv in tiles:
            for s in (1, 2, 4):
                a_sh = jnp.where(sub >= s, pltpu.roll(av, s, 0), 1.0)
                u_sh = jnp.where(sub >= s, pltpu.roll(uv, s, 0), 0.0)
                uv = av * u_sh + uv
                av = av * a_sh
            hv = uv + av * h
            h = hv[nrow - 1:nrow, :]
            outs.append(hv)
        for r0, hv in zip(rows, outs):
            u_ref[pl.ds(r0, nrow), :] = hv
        return h

    lax.fori_loop(0, seq // (nrow * LRU_SCAN_UNROLL), body, jnp.zeros((1, x.shape[1]), F32))
    o_ref[0] = (u_ref[...] * gate_ref[0].astype(F32)).astype(BF16)


def _lru(l, proj3, cw, cb, wa, ba, wx, bx, lam):
    b, seq, _ = proj3.shape
    nblk = LRU_BLOCKS_PER_STEP
    cblk = nblk * LRU_BLOCK
    vec = _layer_spec(l, (1, cblk), lambda i, h: (0, h))
    wspec = _layer_spec(l, (nblk, LRU_BLOCK, LRU_BLOCK), lambda i, h: (h, 0, 0))
    return pl.pallas_call(
        _lru_kernel,
        grid=(b, LRU_BLOCKS // nblk),
        in_specs=[
            pl.BlockSpec((1, seq, cblk), lambda i, h: (i, 0, OFF_LX // cblk + h)),
            pl.BlockSpec((1, seq, cblk), lambda i, h: (i, 0, OFF_LG // cblk + h)),
            _layer_spec(l, (CONV_WIDTH, cblk), lambda i, h: (0, h)),
            vec, wspec, vec, wspec, vec, vec,
        ],
        out_specs=pl.BlockSpec((1, seq, cblk), lambda i, h: (i, 0, h)),
        out_shape=jax.ShapeDtypeStruct((b, seq, LRU_WIDTH), BF16),
        scratch_shapes=[pltpu.VMEM((seq, cblk), F32), pltpu.VMEM((seq, cblk), F32)],
        compiler_params=_params("parallel", "parallel"),
        name="rglru",
    )(proj3, proj3, cw, cb, wa, ba, wx, bx, lam)


def _sb_kernel(q_ref, k_ref, v_ref, o_ref, acc_ref, car_ref, bias_ref):
    t, wk, d = SB_TILE, SB_WINDOW, SB_HEAD_DIM
    nt = SB_TILES_PER_STEP
    step = pl.program_id(1)
    row = lax.broadcasted_iota(jnp.int32, (t, wk), 0)
    col = lax.broadcasted_iota(jnp.int32, (t, wk), 1)
    col_minus_row = col - row
    tri = jnp.where(lax.broadcasted_iota(jnp.int32, (wk, wk), 0)
                    > lax.broadcasted_iota(jnp.int32, (wk, wk), 1), 1.0, 0.0).astype(BF16)

    def hs(h):
        return slice(h * d, (h + 1) * d)

    def rs(s):
        return slice(s * t, (s + 1) * t)

    chains = [(s, h) for s in range(nt) for h in range(SB_HEADS)]

    def window_pass(kstarts, first):
        worst = None
        for g0 in range(0, len(chains), SB_GROUP):
            group = chains[g0:g0 + SB_GROUP]
            old = None if first else [(acc_ref[s * SB_HEADS + h], car_ref[s * SB_HEADS + h])
                                      for s, h in group]
            zs = [lax.dot_general(q_ref[0, rs(s), hs(h)], k_ref[0, pl.ds(kstarts[s], wk), hs(h)],
                                  NT_DIMS, preferred_element_type=F32) + bias_ref[s]
                  for s, h in group]
            lszs = [jnp.minimum(z, 0.0) - jnp.log(1.0 + jnp.exp(-jnp.abs(z))) for z in zs]
            lks = [lsz - z for lsz, z in zip(lszs, zs)]
            afters = [jnp.dot(lk.astype(BF16), tri, preferred_element_type=F32) for lk in lks]
            carries = [jnp.sum(lk, axis=-1, keepdims=True) for lk in lks]
            if not first:
                afters = [a + o[1] for a, o in zip(afters, old)]
                carries = [c + o[1] for c, o in zip(carries, old)]
            ws = [jnp.exp(lsz + after).astype(BF16) for lsz, after in zip(lszs, afters)]
            accs = [jnp.dot(w, v_ref[0, pl.ds(kstarts[s], wk), hs(h)], preferred_element_type=F32)
                    for w, (s, h) in zip(ws, group)]
            if not first:
                accs = [a + o[0] for a, o in zip(accs, old)]
            for (s, h), acc, carry in zip(group, accs, carries):
                acc_ref[s * SB_HEADS + h] = acc
                car_ref[s * SB_HEADS + h] = carry
                if first:
                    o_ref[0, rs(s), hs(h)] = acc.astype(BF16)
                worst = carry if worst is None else jnp.maximum(worst, carry)
        return worst

    def alive(carry):
        return (jnp.max(carry) > SB_DEAD).astype(jnp.int32)

    kstarts = []
    for s in range(nt):
        tile = step * nt + s
        kstart = pl.multiple_of(jnp.maximum(tile - 1, 0) * t, t)
        bias_ref[s] = jnp.where(col_minus_row < tile * t - kstart, 0.0, SB_MASKED)
        kstarts.append(kstart)
    worst = window_pass(kstarts, True)

    @pl.when(jnp.logical_and(alive(worst) > 0, kstarts[nt - 1] > 0))
    def _():
        def cond(st):
            return jnp.logical_and(st[nt - 1] > 0, st[nt] > 0)

        def body(st):
            kends = st[:nt]
            starts = [pl.multiple_of(jnp.maximum(kend - wk, 0), t) for kend in kends]
            for s in range(nt):
                bias_ref[s] = jnp.where(col < kends[s] - starts[s], 0.0, SB_MASKED)
            return tuple(starts) + (alive(window_pass(starts, False)),)

        lax.while_loop(cond, body, tuple(kstarts) + (jnp.int32(1),))
        for s, h in chains:
            o_ref[0, rs(s), hs(h)] = acc_ref[s * SB_HEADS + h].astype(BF16)


def _sb(proj3):
    b, seq, _ = proj3.shape
    rows = SB_TILES_PER_STEP * SB_TILE
    w = SB_HEADS * SB_HEAD_DIM
    nchain = SB_TILES_PER_STEP * SB_HEADS
    return pl.pallas_call(
        _sb_kernel,
        grid=(b, seq // rows),
        in_specs=[
            pl.BlockSpec((1, rows, w), lambda i, s: (i, s, OFF_SQ // w)),
            pl.BlockSpec((1, seq, w), lambda i, s: (i, 0, OFF_SK // w)),
            pl.BlockSpec((1, seq, w), lambda i, s: (i, 0, OFF_SV // w)),
        ],
        out_specs=pl.BlockSpec((1, rows, w), lambda i, s: (i, s, 0)),
        out_shape=jax.ShapeDtypeStruct((b, seq, w), BF16),
        scratch_shapes=[pltpu.VMEM((nchain, SB_TILE, SB_HEAD_DIM), F32),
                        pltpu.VMEM((nchain, SB_TILE, 1), F32),
                        pltpu.VMEM((SB_TILES_PER_STEP, SB_TILE, SB_WINDOW), F32)],
        compiler_params=_params("parallel", "arbitrary"),
        name="stickbreak",
    )(proj3, proj3, proj3)


def _merge_kernel(x_ref, oa_ref, ob_ref, oc_ref, ga_ref, gb_ref, gc_ref, bg_ref,
                  wa_ref, wb_ref, wc_ref, wo_ref, o_ref):
    def branch(o_r, w_r, gl_r, idx):
        gate = _sigmoid(gl_r[...].astype(F32) + bg_ref[:, idx * D_MODEL:(idx + 1) * D_MODEL])
        return gate * jnp.dot(o_r[...], w_r[...].astype(BF16), preferred_element_type=F32)

    merged = (branch(oa_ref, wa_ref, ga_ref, 0) + branch(ob_ref, wb_ref, gb_ref, 1)
              + branch(oc_ref, wc_ref, gc_ref, 2))
    o_ref[...] = x_ref[...] + jnp.dot(merged.astype(BF16), wo_ref[...].astype(BF16),
                                      preferred_element_type=F32)


def _merge(l, x2, oa, ob, oc, proj, bg, wa, wb, wc, wo, tm=512):
    n = x2.shape[0]
    rows = pl.BlockSpec((tm, D_MODEL), lambda i: (i, 0))
    wspec = pl.BlockSpec((None, D_MODEL, D_MODEL), lambda i: (l, 0, 0),
                         pipeline_mode=pl.Buffered(1))
    gate_blk = OFF_GATE // D_MODEL

    def gspec(idx):
        return pl.BlockSpec((tm, D_MODEL), lambda i: (i, gate_blk + idx))

    return pl.pallas_call(
        _merge_kernel,
        grid=(n // tm,),
        in_specs=[rows, rows, rows, rows, gspec(0), gspec(1), gspec(2),
                  _layer_spec(l, (1, 3 * D_MODEL), lambda i: (0, 0)),
                  wspec, wspec, wspec, wspec],
        out_specs=rows,
        out_shape=jax.ShapeDtypeStruct((n, D_MODEL), F32),
        compiler_params=_params("parallel"),
        name="merge_out",
    )(x2, oa, ob, oc, proj, proj, proj, bg, wa, wb, wc, wo)


def _mlp_kernel(x_ref, g_ref, wu_ref, wd_ref, o_ref, hn_ref):
    @pl.when(pl.program_id(1) == 0)
    def _():
        x = x_ref[...]
        hn = x * lax.rsqrt(jnp.mean(x * x, axis=-1, keepdims=True) + EPS) * g_ref[...]
        hn_ref[...] = hn.astype(BF16)
        o_ref[...] = x

    up = jnp.dot(hn_ref[...], wu_ref[...].astype(BF16), preferred_element_type=F32)
    act = jnp.square(jnp.maximum(up, 0.0)).astype(BF16)
    o_ref[...] += jnp.dot(act, wd_ref[...].astype(BF16), preferred_element_type=F32)


def _mlp(l, x2, g, wu, wd, tm=1024, tf=1024):
    n = x2.shape[0]
    return pl.pallas_call(
        _mlp_kernel,
        grid=(n // tm, D_FF // tf),
        in_specs=[
            pl.BlockSpec((tm, D_MODEL), lambda i, f: (i, 0)),
            _layer_spec(l, (1, D_MODEL), lambda i, f: (0, 0)),
            _layer_spec(l, (D_MODEL, tf), lambda i, f: (0, f)),
            _layer_spec(l, (tf, D_MODEL), lambda i, f: (f, 0)),
        ],
        out_specs=pl.BlockSpec((tm, D_MODEL), lambda i, f: (i, 0)),
        out_shape=jax.ShapeDtypeStruct((n, D_MODEL), F32),
        scratch_shapes=[pltpu.VMEM((tm, D_MODEL), BF16)],
        compiler_params=_params("parallel", "arbitrary"),
        name="mlp",
    )(x2, g, wu, wd)


def kernel(x, norm_mix_g, w_in, gla_w_up, gla_b_alpha, gla_norm_g, lru_conv_w, lru_conv_b, lru_w_a,
           lru_b_a, lru_w_x, lru_b_x, lru_lambda, sb_q_norm_g, sb_k_norm_g, w_branch_a, w_branch_b,
           w_branch_c, b_gate, w_out, norm_mlp_g, w_mlp_up, w_mlp_down):
    b, seq, d = x.shape
    depth = w_in.shape[0]
    ad0 = 2 * GLA_DK + 2 * GLA_DV
    w_in = w_in.astype(BF16)
    w_main = jnp.concatenate([w_in[:, :, :ad0], w_in[:, :, ad0 + GLA_RANK:]], axis=-1)
    w_ad = jnp.pad(w_in[:, :, ad0:ad0 + GLA_RANK], ((0, 0), (0, 0), (0, LANE - GLA_RANK)))
    wup = jnp.pad(gla_w_up, ((0, 0), (0, LANE - GLA_RANK), (0, 0))).astype(BF16)
    wa_l, wx_l = lru_w_a.astype(BF16), lru_w_x.astype(BF16)
    wba, wbb, wbc, wo, wu, wd = w_branch_a, w_branch_b, w_branch_c, w_out, w_mlp_up, w_mlp_down

    def vec(p):
        return p[:, None, :]

    mix_g, mlp_g = vec(norm_mix_g), vec(norm_mlp_g)
    b_alpha, gla_ng = vec(gla_b_alpha), vec(gla_norm_g)
    conv_b, b_a, b_x, lam = vec(lru_conv_b), vec(lru_b_a), vec(lru_b_x), vec(lru_lambda)

    ep = jnp.zeros((depth, PROJ_COLS // D_MODEL, D_MODEL), F32)
    ep = ep.at[:, TILE_SQ].set(jnp.tile(sb_q_norm_g * SB_HEAD_DIM ** -0.5, (1, SB_HEADS)))
    ep = ep.at[:, TILE_SK].set(jnp.tile(sb_k_norm_g, (1, SB_HEADS)))
    ep = ep[:, :, None, :]
    bg = vec(b_gate)

    x2 = x.reshape(b * seq, d)
    for l in range(depth):
        proj, ad = _in_proj(l, x2, mix_g, w_main, w_ad, ep)
        proj3 = proj.reshape(b, seq, PROJ_COLS)
        ad3 = ad.reshape(b, seq, LANE)
        o_a = _gla(l, proj3, ad3, wup, b_alpha, gla_ng)
        o_b = _lru(l, proj3, lru_conv_w, conv_b, wa_l, b_a, wx_l, b_x, lam)
        o_c = _sb(proj3)
        x2 = _merge(l, x2, o_a.reshape(b * seq, d), o_b.reshape(b * seq, d), o_c.reshape(b * seq, d),
                    proj, bg, wba, wbb, wbc, wo)
        x2 = _mlp(l, x2, mlp_g, wu, wd)
    return x2.reshape(b, seq, d)
```

```python
import jax
import jax.numpy as jnp
from jax import lax
from jax.experimental import pallas as pl
from jax.experimental.pallas import tpu as pltpu

F32 = jnp.float32
BF16 = jnp.bfloat16

D_MODEL = 1024
GLA_HEADS = 4
GLA_DK = 512
GLA_DV = 1024
GLA_HEAD_DK = GLA_DK // GLA_HEADS
GLA_HEAD_DV = GLA_DV // GLA_HEADS
GLA_RANK = 16
GLA_GATE_TAU = 16.0
GLA_CHUNK = 64
GLA_UNROLL = 32
LRU_WIDTH = 1024
LRU_BLOCKS = 4
LRU_BLOCK = LRU_WIDTH // LRU_BLOCKS
CONV_WIDTH = 4
LRU_C = 8.0
LRU_BLOCKS_PER_STEP = 2
LRU_SCAN_UNROLL = 8
SB_HEAD_DIM = 128
SB_HEADS = 8
SB_TILE = 128
SB_WINDOW = 256
SB_TILES_PER_STEP = 2
SB_GROUP = 8
SB_DEAD = -30.0
SB_MASKED = -1e30
D_FF = 4096
EPS = 1e-6

LANE = 128
SUBLANE = 8
VMEM_LIMIT = 48 * 1024 * 1024

OFF_GQ = 0
OFF_GK = OFF_GQ + GLA_DK
OFF_GV = OFF_GK + GLA_DK
OFF_GG = OFF_GV + GLA_DV
OFF_LX = OFF_GG + GLA_DV
OFF_LG = OFF_LX + LRU_WIDTH
OFF_SQ = OFF_LG + LRU_WIDTH
OFF_SK = OFF_SQ + D_MODEL
OFF_SV = OFF_SK + D_MODEL
OFF_GATE = OFF_SV + D_MODEL
PROJ_COLS = OFF_GATE + 3 * D_MODEL
TILE_LG, TILE_SQ, TILE_SK = OFF_LG // D_MODEL, OFF_SQ // D_MODEL, OFF_SK // D_MODEL

NT_DIMS = (((1,), (1,)), ((), ()))
TN_DIMS = (((0,), (0,)), ((), ()))


def _params(*sem):
    return pltpu.CompilerParams(dimension_semantics=sem, vmem_limit_bytes=VMEM_LIMIT)


def _layer_spec(l, shape, index_map):
    return pl.BlockSpec((None,) + shape, lambda *g: (l,) + index_map(*g))


def _softplus(z):
    return jnp.maximum(z, 0.0) + jnp.log(1.0 + jnp.exp(-jnp.abs(z)))


def _sigmoid(z):
    return 0.5 * jnp.tanh(0.5 * z) + 0.5


def _gelu_tanh(x):
    return 0.5 * x * (1.0 + jnp.tanh(0.7978845608028654 * (x + 0.044715 * (x * x * x))))


def _head_rms_norm(t, gain):
    heads = []
    for h in range(SB_HEADS):
        y = t[:, h * SB_HEAD_DIM:(h + 1) * SB_HEAD_DIM]
        heads.append(y * lax.rsqrt(jnp.mean(y * y, axis=-1, keepdims=True) + EPS))
    return jnp.concatenate(heads, axis=1) * gain


def _in_proj_kernel(x_ref, g_ref, w_ref, wad_ref, ep_ref, proj_ref, ad_ref, xn_ref):
    j = pl.program_id(1)

    @pl.when(j == 0)
    def _():
        x = x_ref[...]
        xn = x * lax.rsqrt(jnp.mean(x * x, axis=-1, keepdims=True) + EPS) * g_ref[...]
        xnb = xn.astype(BF16)
        xn_ref[...] = xnb
        ad_ref[...] = jnp.dot(xnb, wad_ref[...], preferred_element_type=F32)

    def tile(finish):
        acc = jnp.dot(xn_ref[...], w_ref[...], preferred_element_type=F32)
        proj_ref[...] = finish(acc).astype(BF16)

    is_gelu = j == TILE_LG
    is_qk = jnp.logical_or(j == TILE_SQ, j == TILE_SK)
    pl.when(jnp.logical_not(jnp.logical_or(is_gelu, is_qk)))(lambda: tile(lambda a: a))
    pl.when(is_gelu)(lambda: tile(_gelu_tanh))
    pl.when(is_qk)(lambda: tile(lambda a: _head_rms_norm(a, ep_ref[...])))


def _in_proj(l, x2, g, w_main, w_ad, ep, tm=2048, tn=D_MODEL):
    n = x2.shape[0]
    return pl.pallas_call(
        _in_proj_kernel,
        grid=(n // tm, PROJ_COLS // tn),
        in_specs=[
            pl.BlockSpec((tm, D_MODEL), lambda i, j: (i, 0)),
            _layer_spec(l, (1, D_MODEL), lambda i, j: (0, 0)),
            _layer_spec(l, (D_MODEL, tn), lambda i, j: (0, j)),
            _layer_spec(l, (D_MODEL, LANE), lambda i, j: (0, 0)),
            pl.BlockSpec((None, None, 1, tn), lambda i, j: (l, j, 0, 0)),
        ],
        out_specs=[
            pl.BlockSpec((tm, tn), lambda i, j: (i, j)),
            pl.BlockSpec((tm, LANE), lambda i, j: (i, 0)),
        ],
        out_shape=[
            jax.ShapeDtypeStruct((n, PROJ_COLS), BF16),
            jax.ShapeDtypeStruct((n, LANE), F32),
        ],
        scratch_shapes=[pltpu.VMEM((tm, D_MODEL), BF16)],
        compiler_params=_params("parallel", "arbitrary"),
        name="in_proj",
    )(x2, g, w_main, w_ad, ep)


def _gla_kernel(q_ref, k_ref, v_ref, g_ref, ad_ref, wup_ref, ba_ref, ng_ref, o_ref,
                la_ref, qd_ref, oi_ref, kv_ref, dec_ref):
    seq = q_ref.shape[1]
    c = GLA_CHUNK
    logits = jnp.dot(ad_ref[0].astype(BF16), wup_ref[...], preferred_element_type=F32) + ba_ref[...]
    la_ref[...] = -_softplus(-logits) * (1.0 / GLA_GATE_TAU)

    row = lax.broadcasted_iota(jnp.int32, (c, c), 0)
    col = lax.broadcasted_iota(jnp.int32, (c, c), 1)
    causal = row >= col
    tri = jnp.where(causal, 1.0, 0.0).astype(BF16)
    scale = GLA_HEAD_DK ** -0.5

    def chunk_local(gi, carry):
        rows = [pl.multiple_of((gi * GLA_UNROLL + u) * c, c) for u in range(GLA_UNROLL)]
        loaded = [(la_ref[pl.ds(r0, c), :], q_ref[0, pl.ds(r0, c), :], k_ref[0, pl.ds(r0, c), :],
                   v_ref[0, pl.ds(r0, c), :]) for r0 in rows]
        cums = [jnp.dot(tri, la.astype(BF16), preferred_element_type=F32)
                for la, _, _, _ in loaded]
        scaled = []
        for cum, (_, q, k, _) in zip(cums, loaded):
            last = cum[c - 1:c, :]
            k = k.astype(F32)
            scaled.append(((q.astype(F32) * scale * jnp.exp(cum)).astype(BF16),
                           (k * jnp.exp(-cum)).astype(BF16),
                           (k * jnp.exp(last - cum)).astype(BF16),
                           jnp.broadcast_to(jnp.exp(last), (SUBLANE, GLA_HEAD_DK))))
        scores = [lax.dot_general(q_dec, k_inv, NT_DIMS, preferred_element_type=F32)
                  for q_dec, k_inv, _, _ in scaled]
        kvs = [lax.dot_general(v, k_end, TN_DIMS, preferred_element_type=F32)
               for (_, _, k_end, _), (_, _, _, v) in zip(scaled, loaded)]
        scores = [jnp.where(causal, s, 0.0).astype(BF16) for s in scores]
        results = [(q_dec, jnp.dot(s, v, preferred_element_type=F32), kv, dec)
                   for s, kv, (q_dec, _, _, dec), (_, _, _, v) in zip(scores, kvs, scaled, loaded)]
        for u, (q_dec, o_intra, kv, dec) in enumerate(results):
            qd_ref[pl.ds(rows[u], c), :] = q_dec
            oi_ref[pl.ds(rows[u], c), :] = o_intra
            kv_ref[gi * GLA_UNROLL + u] = kv
            dec_ref[gi * GLA_UNROLL + u] = dec
        return carry

    lax.fori_loop(0, seq // (c * GLA_UNROLL), chunk_local, 0)

    def chunk_state(gi, st):
        rows = [pl.multiple_of((gi * GLA_UNROLL + u) * c, c) for u in range(GLA_UNROLL)]
        loaded = [(oi_ref[pl.ds(r0, c), :], qd_ref[pl.ds(r0, c), :], g_ref[0, pl.ds(r0, c), :],
                   kv_ref[gi * GLA_UNROLL + u], dec_ref[gi * GLA_UNROLL + u])
                  for u, r0 in enumerate(rows)]
        outs = []
        for o_intra, q_dec, g, kv, dec in loaded:
            o = o_intra + lax.dot_general(q_dec, st.astype(BF16), NT_DIMS,
                                          preferred_element_type=F32)
            st = st * dec[0:1, :] + kv
            o = o * lax.rsqrt(jnp.mean(o * o, axis=-1, keepdims=True) + EPS) * ng_ref[...]
            g = g.astype(F32)
            outs.append((o * (g * _sigmoid(g))).astype(BF16))
        for u, o in enumerate(outs):
            o_ref[0, pl.ds(rows[u], c), :] = o
        return st

    lax.fori_loop(0, seq // (c * GLA_UNROLL), chunk_state,
                  jnp.zeros((GLA_HEAD_DV, GLA_HEAD_DK), F32))


def _gla(l, proj3, ad3, wup, ba, ng):
    b, seq, _ = proj3.shape
    dk, dv = GLA_HEAD_DK, GLA_HEAD_DV
    return pl.pallas_call(
        _gla_kernel,
        grid=(b, GLA_HEADS),
        in_specs=[
            pl.BlockSpec((1, seq, dk), lambda i, h: (i, 0, OFF_GQ // dk + h)),
            pl.BlockSpec((1, seq, dk), lambda i, h: (i, 0, OFF_GK // dk + h)),
            pl.BlockSpec((1, seq, dv), lambda i, h: (i, 0, OFF_GV // dv + h)),
            pl.BlockSpec((1, seq, dv), lambda i, h: (i, 0, OFF_GG // dv + h)),
            pl.BlockSpec((1, seq, LANE), lambda i, h: (i, 0, 0)),
            _layer_spec(l, (LANE, dk), lambda i, h: (0, h)),
            _layer_spec(l, (1, dk), lambda i, h: (0, h)),
            _layer_spec(l, (1, dv), lambda i, h: (0, 0)),
        ],
        out_specs=pl.BlockSpec((1, seq, dv), lambda i, h: (i, 0, h)),
        out_shape=jax.ShapeDtypeStruct((b, seq, GLA_DV), BF16),
        scratch_shapes=[
            pltpu.VMEM((seq, dk), F32),
            pltpu.VMEM((seq, dk), BF16),
            pltpu.VMEM((seq, dv), F32),
            pltpu.VMEM((seq // GLA_CHUNK, dv, dk), F32),
            pltpu.VMEM((seq // GLA_CHUNK, SUBLANE, dk), F32),
        ],
        compiler_params=_params("parallel", "parallel"),
        name="gla",
    )(proj3, proj3, proj3, proj3, ad3, wup, ba, ng)


def _lru_kernel(x_ref, gate_ref, cw_ref, cb_ref, wa_ref, ba_ref, wx_ref, bx_ref, lam_ref, o_ref,
                a_ref, u_ref):
    seq = x_ref.shape[1]
    x = x_ref[0].astype(F32)
    nrow = SUBLANE
    sub = lax.broadcasted_iota(jnp.int32, (nrow, x.shape[1]), 0)

    def conv(xs, shifted):
        out = xs * cw_ref[CONV_WIDTH - 1:CONV_WIDTH, :] + cb_ref[...]
        for s in range(1, CONV_WIDTH):
            out = out + shifted(s) * cw_ref[CONV_WIDTH - 1 - s:CONV_WIDTH - s, :]
        return out

    xc = conv(x, lambda s: pltpu.roll(x, s, 0))
    head = x[0:nrow, :]
    xc_head = conv(head, lambda s: jnp.where(sub >= s, pltpu.roll(head, s, 0), 0.0))
    xc = jnp.concatenate([xc_head, xc[nrow:, :]], axis=0)

    xcb = xc.astype(BF16)
    neg_c_sp = (-LRU_C) * _softplus(-lam_ref[...])
    for blk in range(wa_ref.shape[0]):
        cs = slice(blk * LRU_BLOCK, (blk + 1) * LRU_BLOCK)
        xb = xcb[:, cs]
        r = _sigmoid(jnp.dot(xb, wa_ref[blk], preferred_element_type=F32) + ba_ref[:, cs])
        gi = _sigmoid(jnp.dot(xb, wx_ref[blk], preferred_element_type=F32) + bx_ref[:, cs])
        a = jnp.exp(r * neg_c_sp[:, cs])
        y = 1.0 - a * a
        root = jnp.where(y > 0.0, y * lax.rsqrt(y), 0.0)
        a_ref[:, cs] = a
        u_ref[:, cs] = root * gi * xc[:, cs]

    def body(gi, h):
        rows = [pl.multiple_of((gi * LRU_SCAN_UNROLL + j) * nrow, nrow)
                for j in range(LRU_SCAN_UNROLL)]
        tiles = [(a_ref[pl.ds(r0, nrow), :], u_ref[pl.ds(r0, nrow), :]) for r0 in rows]
        outs = []
        for av, uv in tiles:
            for s in (1, 2, 4):
                a_sh = jnp.where(sub >= s, pltpu.roll(av, s, 0), 1.0)
                u_sh = jnp.where(sub >= s, pltpu.roll(uv, s, 0), 0.0)
                uv = av * u_sh + uv
                av = av * a_sh
            hv = uv + av * h
            h = hv[nrow - 1:nrow, :]
            outs.append(hv)
        for r0, hv in zip(rows, outs):
            u_ref[pl.ds(r0, nrow), :] = hv
        return h

    lax.fori_loop(0, seq // (nrow * LRU_SCAN_UNROLL), body, jnp.zeros((1, x.shape[1]), F32))
    o_ref[0] = (u_ref[...] * gate_ref[0].astype(F32)).astype(BF16)


def _lru(l, proj3, cw, cb, wa, ba, wx, bx, lam):
    b, seq, _ = proj3.shape
    nblk = LRU_BLOCKS_PER_STEP
    cblk = nblk * LRU_BLOCK
    vec = _layer_spec(l, (1, cblk), lambda i, h: (0, h))
    wspec = _layer_spec(l, (nblk, LRU_BLOCK, LRU_BLOCK), lambda i, h: (h, 0, 0))
    return pl.pallas_call(
        _lru_kernel,
        grid=(b, LRU_BLOCKS // nblk),
        in_specs=[
            pl.BlockSpec((1, seq, cblk), lambda i, h: (i, 0, OFF_LX // cblk + h)),
            pl.BlockSpec((1, seq, cblk), lambda i, h: (i, 0, OFF_LG // cblk + h)),
            _layer_spec(l, (CONV_WIDTH, cblk), lambda i, h: (0, h)),
            vec, wspec, vec, wspec, vec, vec,
        ],
        out_specs=pl.BlockSpec((1, seq, cblk), lambda i, h: (i, 0, h)),
        out_shape=jax.ShapeDtypeStruct((b, seq, LRU_WIDTH), BF16),
        scratch_shapes=[pltpu.VMEM((seq, cblk), F32), pltpu.VMEM((seq, cblk), F32)],
        compiler_params=_params("parallel", "parallel"),
        name="rglru",
    )(proj3, proj3, cw, cb, wa, ba, wx, bx, lam)


def _sb_kernel(q_ref, k_ref, v_ref, o_ref, acc_ref, car_ref, bias_ref):
    t, wk, d = SB_TILE, SB_WINDOW, SB_HEAD_DIM
    nt = SB_TILES_PER_STEP
    step = pl.program_id(1)
    row = lax.broadcasted_iota(jnp.int32, (t, wk), 0)
    col = lax.broadcasted_iota(jnp.int32, (t, wk), 1)
    col_minus_row = col - row
    tri = jnp.where(lax.broadcasted_iota(jnp.int32, (wk, wk), 0)
                    > lax.broadcasted_iota(jnp.int32, (wk, wk), 1), 1.0, 0.0).astype(BF16)

    def hs(h):
        return slice(h * d, (h + 1) * d)

    def rs(s):
        return slice(s * t, (s + 1) * t)

    chains = [(s, h) for s in range(nt) for h in range(SB_HEADS)]

    def window_pass(kstarts, first):
        worst = None
        for g0 in range(0, len(chains), SB_GROUP):
            group = chains[g0:g0 + SB_GROUP]
            old = None if first else [(acc_ref[s * SB_HEADS + h], car_ref[s * SB_HEADS + h])
                                      for s, h in group]
            zs = [lax.dot_general(q_ref[0, rs(s), hs(h)], k_ref[0, pl.ds(kstarts[s], wk), hs(h)],
                                  NT_DIMS, preferred_element_type=F32) + bias_ref[s]
                  for s, h in group]
            lszs = [jnp.minimum(z, 0.0) - jnp.log(1.0 + jnp.exp(-jnp.abs(z))) for z in zs]
            lks = [lsz - z for lsz, z in zip(lszs, zs)]
            afters = [jnp.dot(lk.astype(BF16), tri, preferred_element_type=F32) for lk in lks]
            carries = [jnp.sum(lk, axis=-1, keepdims=True) for lk in lks]
            if not first:
                afters = [a + o[1] for a, o in zip(afters, old)]
                carries = [c + o[1] for c, o in zip(carries, old)]
            ws = [jnp.exp(lsz + after).astype(BF16) for lsz, after in zip(lszs, afters)]
            accs = [jnp.dot(w, v_ref[0, pl.ds(kstarts[s], wk), hs(h)], preferred_element_type=F32)
                    for w, (s, h) in zip(ws, group)]
            if not first:
                accs = [a + o[0] for a, o in zip(accs, old)]
            for (s, h), acc, carry in zip(group, accs, carries):
                acc_ref[s * SB_HEADS + h] = acc
                car_ref[s * SB_HEADS + h] = carry
                if first:
                    o_ref[0, rs(s), hs(h)] = acc.astype(BF16)
                worst = carry if worst is None else jnp.maximum(worst, carry)
        return worst

    def alive(carry):
        return (jnp.max(carry) > SB_DEAD).astype(jnp.int32)

    kstarts = []
    for s in range(nt):
        tile = step * nt + s
        kstart = pl.multiple_of(jnp.maximum(tile - 1, 0) * t, t)
        bias_ref[s] = jnp.where(col_minus_row < tile * t - kstart, 0.0, SB_MASKED)
        kstarts.append(kstart)
    worst = window_pass(kstarts, True)

    @pl.when(jnp.logical_and(alive(worst) > 0, kstarts[nt - 1] > 0))
    def _():
        def cond(st):
            return jnp.logical_and(st[nt - 1] > 0, st[nt] > 0)

        def body(st):
            kends = st[:nt]
            starts = [pl.multiple_of(jnp.maximum(kend - wk, 0), t) for kend in kends]
            for s in range(nt):
                bias_ref[s] = jnp.where(col < kends[s] - starts[s], 0.0, SB_MASKED)
            return tuple(starts) + (alive(window_pass(starts, False)),)

        lax.while_loop(cond, body, tuple(kstarts) + (jnp.int32(1),))
        for s, h in chains:
            o_ref[0, rs(s), hs(h)] = acc_ref[s * SB_HEADS + h].astype(BF16)


def _sb(proj3):
    b, seq, _ = proj3.shape
    rows = SB_TILES_PER_STEP * SB_TILE
    w = SB_HEADS * SB_HEAD_DIM
    nchain = SB_TILES_PER_STEP * SB_HEADS
    return pl.pallas_call(
        _sb_kernel,
        grid=(b, seq // rows),
        in_specs=[
            pl.BlockSpec((1, rows, w), lambda i, s: (i, s, OFF_SQ // w)),
            pl.BlockSpec((1, seq, w), lambda i, s: (i, 0, OFF_SK // w)),
            pl.BlockSpec((1, seq, w), lambda i, s: (i, 0, OFF_SV // w)),
        ],
        out_specs=pl.BlockSpec((1, rows, w), lambda i, s: (i, s, 0)),
        out_shape=jax.ShapeDtypeStruct((b, seq, w), BF16),
        scratch_shapes=[pltpu.VMEM((nchain, SB_TILE, SB_HEAD_DIM), F32),
                        pltpu.VMEM((nchain, SB_TILE, 1), F32),
                        pltpu.VMEM((SB_TILES_PER_STEP, SB_TILE, SB_WINDOW), F32)],
        compiler_params=_params("parallel", "arbitrary"),
        name="stickbreak",
    )(proj3, proj3, proj3)


def _merge_kernel(x_ref, oa_ref, ob_ref, oc_ref, ga_ref, gb_ref, gc_ref, bg_ref,
                  wa_ref, wb_ref, wc_ref, wo_ref, o_ref):
    def branch(o_r, w_r, gl_r, idx):
        gate = _sigmoid(gl_r[...].astype(F32) + bg_ref[:, idx * D_MODEL:(idx + 1) * D_MODEL])
        return gate * jnp.dot(o_r[...], w_r[...].astype(BF16), preferred_element_type=F32)

    merged = (branch(oa_ref, wa_ref, ga_ref, 0) + branch(ob_ref, wb_ref, gb_ref, 1)
              + branch(oc_ref, wc_ref, gc_ref, 2))
    o_ref[...] = x_ref[...] + jnp.dot(merged.astype(BF16), wo_ref[...].astype(BF16),
                                      preferred_element_type=F32)


def _merge(l, x2, oa, ob, oc, proj, bg, wa, wb, wc, wo, tm=512):
    n = x2.shape[0]
    rows = pl.BlockSpec((tm, D_MODEL), lambda i: (i, 0))
    wspec = pl.BlockSpec((None, D_MODEL, D_MODEL), lambda i: (l, 0, 0),
                         pipeline_mode=pl.Buffered(1))
    gate_blk = OFF_GATE // D_MODEL

    def gspec(idx):
        return pl.BlockSpec((tm, D_MODEL), lambda i: (i, gate_blk + idx))

    return pl.pallas_call(
        _merge_kernel,
        grid=(n // tm,),
        in_specs=[rows, rows, rows, rows, gspec(0), gspec(1), gspec(2),
                  _layer_spec(l, (1, 3 * D_MODEL), lambda i: (0, 0)),
                  wspec, wspec, wspec, wspec],
        out_specs=rows,
        out_shape=jax.ShapeDtypeStruct((n, D_MODEL), F32),
        compiler_params=_params("parallel"),
        name="merge_out",
    )(x2, oa, ob, oc, proj, proj, proj, bg, wa, wb, wc, wo)


def _mlp_kernel(x_ref, g_ref, wu_ref, wd_ref, o_ref, hn_ref):
    @pl.when(pl.program_id(1) == 0)
    def _():
        x = x_ref[...]
        hn = x * lax.rsqrt(jnp.mean(x * x, axis=-1, keepdims=True) + EPS) * g_ref[...]
        hn_ref[...] = hn.astype(BF16)
        o_ref[...] = x

    up = jnp.dot(hn_ref[...], wu_ref[...].astype(BF16), preferred_element_type=F32)
    act = jnp.square(jnp.maximum(up, 0.0)).astype(BF16)
    o_ref[...] += jnp.dot(act, wd_ref[...].astype(BF16), preferred_element_type=F32)


def _mlp(l, x2, g, wu, wd, tm=1024, tf=1024):
    n = x2.shape[0]
    return pl.pallas_call(
        _mlp_kernel,
        grid=(n // tm, D_FF // tf),
        in_specs=[
            pl.BlockSpec((tm, D_MODEL), lambda i, f: (i, 0)),
            _layer_spec(l, (1, D_MODEL), lambda i, f: (0, 0)),
            _layer_spec(l, (D_MODEL, tf), lambda i, f: (0, f)),
            _layer_spec(l, (tf, D_MODEL), lambda i, f: (f, 0)),
        ],
        out_specs=pl.BlockSpec((tm, D_MODEL), lambda i, f: (i, 0)),
        out_shape=jax.ShapeDtypeStruct((n, D_MODEL), F32),
        scratch_shapes=[pltpu.VMEM((tm, D_MODEL), BF16)],
        compiler_params=_params("parallel", "arbitrary"),
        name="mlp",
    )(x2, g, wu, wd)


def kernel(x, norm_mix_g, w_in, gla_w_up, gla_b_alpha, gla_norm_g, lru_conv_w, lru_conv_b, lru_w_a,
           lru_b_a, lru_w_x, lru_b_x, lru_lambda, sb_q_norm_g, sb_k_norm_g, w_branch_a, w_branch_b,
           w_branch_c, b_gate, w_out, norm_mlp_g, w_mlp_up, w_mlp_down):
    b, seq, d = x.shape
    depth = w_in.shape[0]
    ad0 = 2 * GLA_DK + 2 * GLA_DV
    w_in = w_in.astype(BF16)
    w_main = jnp.concatenate([w_in[:, :, :ad0], w_in[:, :, ad0 + GLA_RANK:]], axis=-1)
    w_ad = jnp.pad(w_in[:, :, ad0:ad0 + GLA_RANK], ((0, 0), (0, 0), (0, LANE - GLA_RANK)))
    wup = jnp.pad(gla_w_up, ((0, 0), (0, LANE - GLA_RANK), (0, 0))).astype(BF16)
    wa_l, wx_l = lru_w_a.astype(BF16), lru_w_x.astype(BF16)
    wba, wbb, wbc, wo, wu, wd = w_branch_a, w_branch_b, w_branch_c, w_out, w_mlp_up, w_mlp_down

    def vec(p):
        return p[:, None, :]

    mix_g, mlp_g = vec(norm_mix_g), vec(norm_mlp_g)
    b_alpha, gla_ng = vec(gla_b_alpha), vec(gla_norm_g)
    conv_b, b_a, b_x, lam = vec(lru_conv_b), vec(lru_b_a), vec(lru_b_x), vec(lru_lambda)

    ep = jnp.zeros((depth, PROJ_COLS // D_MODEL, D_MODEL), F32)
    ep = ep.at[:, TILE_SQ].set(jnp.tile(sb_q_norm_g * SB_HEAD_DIM ** -0.5, (1, SB_HEADS)))
    ep = ep.at[:, TILE_SK].set(jnp.tile(sb_k_norm_g, (1, SB_HEADS)))
    ep = ep[:, :, None, :]
    bg = vec(b_gate)

    x2 = x.reshape(b * seq, d)
    for l in range(depth):
        proj, ad = _in_proj(l, x2, mix_g, w_main, w_ad, ep)
        proj3 = proj.reshape(b, seq, PROJ_COLS)
        ad3 = ad.reshape(b, seq, LANE)
        o_a = _gla(l, proj3, ad3, wup, b_alpha, gla_ng)
        o_b = _lru(l, proj3, lru_conv_w, conv_b, wa_l, b_a, wx_l, b_x, lam)
        o_c = _sb(proj3)
        x2 = _merge(l, x2, o_a.reshape(b * seq, d), o_b.reshape(b * seq, d), o_c.reshape(b * seq, d),
                    proj, bg, wba, wbb, wbc, wo)
        x2 = _mlp(l, x2, mlp_g, wu, wd)
    return x2.reshape(b, seq, d)
```

```python
import jax
import jax.numpy as jnp
from jax import lax
from jax.experimental import pallas as pl
from jax.experimental.pallas import tpu as pltpu

F32 = jnp.float32
BF16 = jnp.bfloat16

D_MODEL = 1024
GLA_HEADS = 4
GLA_DK = 512
GLA_DV = 1024
GLA_HEAD_DK = GLA_DK // GLA_HEADS
GLA_HEAD_DV = GLA_DV // GLA_HEADS
GLA_RANK = 16
GLA_GATE_TAU = 16.0
GLA_CHUNK = 64
GLA_UNROLL = 32
LRU_WIDTH = 1024
LRU_BLOCKS = 4
LRU_BLOCK = LRU_WIDTH // LRU_BLOCKS
CONV_WIDTH = 4
LRU_C = 8.0
LRU_BLOCKS_PER_STEP = 2
LRU_SCAN_UNROLL = 8
SB_HEAD_DIM = 128
SB_HEADS = 8
SB_TILE = 128
SB_WINDOW = 256
SB_TILES_PER_STEP = 2
SB_GROUP = 8
SB_DEAD = -30.0
SB_MASKED = -1e30
D_FF = 4096
EPS = 1e-6

LANE = 128
SUBLANE = 8
VMEM_LIMIT = 48 * 1024 * 1024

OFF_GQ = 0
OFF_GK = OFF_GQ + GLA_DK
OFF_GV = OFF_GK + GLA_DK
OFF_GG = OFF_GV + GLA_DV
OFF_LX = OFF_GG + GLA_DV
OFF_LG = OFF_LX + LRU_WIDTH
OFF_SQ = OFF_LG + LRU_WIDTH
OFF_SK = OFF_SQ + D_MODEL
OFF_SV = OFF_SK + D_MODEL
OFF_GATE = OFF_SV + D_MODEL
PROJ_COLS = OFF_GATE + 3 * D_MODEL
TILE_LG, TILE_SQ, TILE_SK = OFF_LG // D_MODEL, OFF_SQ // D_MODEL, OFF_SK // D_MODEL
TILES_LO = OFF_LX // D_MODEL

NT_DIMS = (((1,), (1,)), ((), ()))
TN_DIMS = (((0,), (0,)), ((), ()))


def _params(*sem):
    return pltpu.CompilerParams(dimension_semantics=sem, vmem_limit_bytes=VMEM_LIMIT)


def _layer_spec(l, shape, index_map):
    return pl.BlockSpec((None,) + shape, lambda *g: (l,) + index_map(*g))


def _softplus(z):
    return jnp.maximum(z, 0.0) + jnp.log(1.0 + jnp.exp(-jnp.abs(z)))


def _sigmoid(z):
    return 0.5 * jnp.tanh(0.5 * z) + 0.5


def _gelu_tanh(x):
    return 0.5 * x * (1.0 + jnp.tanh(0.7978845608028654 * (x + 0.044715 * (x * x * x))))


def _head_rms_norm(t, gain):
    heads = []
    for h in range(SB_HEADS):
        y = t[:, h * SB_HEAD_DIM:(h + 1) * SB_HEAD_DIM]
        heads.append(y * lax.rsqrt(jnp.mean(y * y, axis=-1, keepdims=True) + EPS))
    return jnp.concatenate(heads, axis=1) * gain


def _in_proj_kernel(x_ref, g_ref, wlo_ref, whi_ref, wad_ref, ep_ref, proj_ref, ad_ref, xn_ref):
    j = pl.program_id(1)

    @pl.when(j == 0)
    def _():
        x = x_ref[...]
        xn = x * lax.rsqrt(jnp.mean(x * x, axis=-1, keepdims=True) + EPS) * g_ref[...]
        xnb = xn.astype(BF16)
        xn_ref[...] = xnb
        ad_ref[...] = jnp.dot(xnb, wad_ref[...], preferred_element_type=F32)

    def tile(w_ref, finish):
        acc = jnp.dot(xn_ref[...], w_ref[...], preferred_element_type=F32)
        proj_ref[...] = finish(acc).astype(BF16)

    is_lo = j < TILES_LO
    is_gelu = j == TILE_LG
    is_qk = jnp.logical_or(j == TILE_SQ, j == TILE_SK)
    plain_hi = jnp.logical_not(jnp.logical_or(is_lo, jnp.logical_or(is_gelu, is_qk)))
    pl.when(is_lo)(lambda: tile(wlo_ref, lambda a: a))
    pl.when(plain_hi)(lambda: tile(whi_ref, lambda a: a))
    pl.when(is_gelu)(lambda: tile(whi_ref, _gelu_tanh))
    pl.when(is_qk)(lambda: tile(whi_ref, lambda a: _head_rms_norm(a, ep_ref[...])))


def _in_proj(l, x2, g, w_lo, w_hi, w_ad, ep, tm=2048, tn=D_MODEL):
    n = x2.shape[0]
    return pl.pallas_call(
        _in_proj_kernel,
        grid=(n // tm, PROJ_COLS // tn),
        in_specs=[
            pl.BlockSpec((tm, D_MODEL), lambda i, j: (i, 0)),
            _layer_spec(l, (1, D_MODEL), lambda i, j: (0, 0)),
            _layer_spec(l, (D_MODEL, tn), lambda i, j: (0, jnp.minimum(j, TILES_LO - 1))),
            _layer_spec(l, (D_MODEL, tn), lambda i, j: (0, jnp.maximum(j - TILES_LO, 0))),
            _layer_spec(l, (D_MODEL, LANE), lambda i, j: (0, 0)),
            pl.BlockSpec((None, None, 1, tn), lambda i, j: (l, j, 0, 0)),
        ],
        out_specs=[
            pl.BlockSpec((tm, tn), lambda i, j: (i, j)),
            pl.BlockSpec((tm, LANE), lambda i, j: (i, 0)),
        ],
        out_shape=[
            jax.ShapeDtypeStruct((n, PROJ_COLS), BF16),
            jax.ShapeDtypeStruct((n, LANE), F32),
        ],
        scratch_shapes=[pltpu.VMEM((tm, D_MODEL), BF16)],
        compiler_params=_params("parallel", "arbitrary"),
        name="in_proj",
    )(x2, g, w_lo, w_hi, w_ad, ep)


def _gla_kernel(q_ref, k_ref, v_ref, g_ref, ad_ref, wup_ref, ba_ref, ng_ref, o_ref,
                la_ref, qd_ref, oi_ref, kv_ref, dec_ref):
    seq = q_ref.shape[1]
    c = GLA_CHUNK
    logits = jnp.dot(ad_ref[0].astype(BF16), wup_ref[...], preferred_element_type=F32) + ba_ref[...]
    la_ref[...] = -_softplus(-logits) * (1.0 / GLA_GATE_TAU)

    row = lax.broadcasted_iota(jnp.int32, (c, c), 0)
    col = lax.broadcasted_iota(jnp.int32, (c, c), 1)
    causal = row >= col
    tri = jnp.where(causal, 1.0, 0.0).astype(BF16)
    scale = GLA_HEAD_DK ** -0.5

    def chunk_local(gi, carry):
        rows = [pl.multiple_of((gi * GLA_UNROLL + u) * c, c) for u in range(GLA_UNROLL)]
        loaded = [(la_ref[pl.ds(r0, c), :], q_ref[0, pl.ds(r0, c), :], k_ref[0, pl.ds(r0, c), :],
                   v_ref[0, pl.ds(r0, c), :]) for r0 in rows]
        cums = [jnp.dot(tri, la.astype(BF16), preferred_element_type=F32)
                for la, _, _, _ in loaded]
        scaled = []
        for cum, (_, q, k, _) in zip(cums, loaded):
            last = cum[c - 1:c, :]
            k = k.astype(F32)
            scaled.append(((q.astype(F32) * scale * jnp.exp(cum)).astype(BF16),
                           (k * jnp.exp(-cum)).astype(BF16),
                           (k * jnp.exp(last - cum)).astype(BF16),
                           jnp.broadcast_to(jnp.exp(last), (SUBLANE, GLA_HEAD_DK))))
        scores = [lax.dot_general(q_dec, k_inv, NT_DIMS, preferred_element_type=F32)
                  for q_dec, k_inv, _, _ in scaled]
        kvs = [lax.dot_general(v, k_end, TN_DIMS, preferred_element_type=F32)
               for (_, _, k_end, _), (_, _, _, v) in zip(scaled, loaded)]
        scores = [jnp.where(causal, s, 0.0).astype(BF16) for s in scores]
        results = [(q_dec, jnp.dot(s, v, preferred_element_type=F32), kv, dec)
                   for s, kv, (q_dec, _, _, dec), (_, _, _, v) in zip(scores, kvs, scaled, loaded)]
        for u, (q_dec, o_intra, kv, dec) in enumerate(results):
            qd_ref[pl.ds(rows[u], c), :] = q_dec
            oi_ref[pl.ds(rows[u], c), :] = o_intra
            kv_ref[gi * GLA_UNROLL + u] = kv
            dec_ref[gi * GLA_UNROLL + u] = dec
        return carry

    lax.fori_loop(0, seq // (c * GLA_UNROLL), chunk_local, 0)

    def chunk_state(gi, st):
        rows = [pl.multiple_of((gi * GLA_UNROLL + u) * c, c) for u in range(GLA_UNROLL)]
        loaded = [(oi_ref[pl.ds(r0, c), :], qd_ref[pl.ds(r0, c), :], g_ref[0, pl.ds(r0, c), :],
                   kv_ref[gi * GLA_UNROLL + u], dec_ref[gi * GLA_UNROLL + u])
                  for u, r0 in enumerate(rows)]
        outs = []
        for o_intra, q_dec, g, kv, dec in loaded:
            o = o_intra + lax.dot_general(q_dec, st.astype(BF16), NT_DIMS,
                                          preferred_element_type=F32)
            st = st * dec[0:1, :] + kv
            o = o * lax.rsqrt(jnp.mean(o * o, axis=-1, keepdims=True) + EPS) * ng_ref[...]
            g = g.astype(F32)
            outs.append((o * (g * _sigmoid(g))).astype(BF16))
        for u, o in enumerate(outs):
            o_ref[0, pl.ds(rows[u], c), :] = o
        return st

    lax.fori_loop(0, seq // (c * GLA_UNROLL), chunk_state,
                  jnp.zeros((GLA_HEAD_DV, GLA_HEAD_DK), F32))


def _gla(l, proj3, ad3, wup, ba, ng):
    b, seq, _ = proj3.shape
    dk, dv = GLA_HEAD_DK, GLA_HEAD_DV
    return pl.pallas_call(
        _gla_kernel,
        grid=(b, GLA_HEADS),
        in_specs=[
            pl.BlockSpec((1, seq, dk), lambda i, h: (i, 0, OFF_GQ // dk + h)),
            pl.BlockSpec((1, seq, dk), lambda i, h: (i, 0, OFF_GK // dk + h)),
            pl.BlockSpec((1, seq, dv), lambda i, h: (i, 0, OFF_GV // dv + h)),
            pl.BlockSpec((1, seq, dv), lambda i, h: (i, 0, OFF_GG // dv + h)),
            pl.BlockSpec((1, seq, LANE), lambda i, h: (i, 0, 0)),
            _layer_spec(l, (LANE, dk), lambda i, h: (0, h)),
            _layer_spec(l, (1, dk), lambda i, h: (0, h)),
            _layer_spec(l, (1, dv), lambda i, h: (0, 0)),
        ],
        out_specs=pl.BlockSpec((1, seq, dv), lambda i, h: (i, 0, h)),
        out_shape=jax.ShapeDtypeStruct((b, seq, GLA_DV), BF16),
        scratch_shapes=[
            pltpu.VMEM((seq, dk), F32),
            pltpu.VMEM((seq, dk), BF16),
            pltpu.VMEM((seq, dv), F32),
            pltpu.VMEM((seq // GLA_CHUNK, dv, dk), F32),
            pltpu.VMEM((seq // GLA_CHUNK, SUBLANE, dk), F32),
        ],
        compiler_params=_params("parallel", "parallel"),
        name="gla",
    )(proj3, proj3, proj3, proj3, ad3, wup, ba, ng)


def _lru_kernel(x_ref, gate_ref, cw_ref, cb_ref, wa_ref, ba_ref, wx_ref, bx_ref, lam_ref, o_ref,
                a_ref, u_ref):
    seq = x_ref.shape[1]
    x = x_ref[0].astype(F32)
    nrow = SUBLANE
    sub = lax.broadcasted_iota(jnp.int32, (nrow, x.shape[1]), 0)

    def conv(xs, shifted):
        out = xs * cw_ref[CONV_WIDTH - 1:CONV_WIDTH, :] + cb_ref[...]
        for s in range(1, CONV_WIDTH):
            out = out + shifted(s) * cw_ref[CONV_WIDTH - 1 - s:CONV_WIDTH - s, :]
        return out

    xc = conv(x, lambda s: pltpu.roll(x, s, 0))
    head = x[0:nrow, :]
    xc_head = conv(head, lambda s: jnp.where(sub >= s, pltpu.roll(head, s, 0), 0.0))
    xc = jnp.concatenate([xc_head, xc[nrow:, :]], axis=0)

    xcb = xc.astype(BF16)
    neg_c_sp = (-LRU_C) * _softplus(-lam_ref[...])
    for blk in range(wa_ref.shape[0]):
        cs = slice(blk * LRU_BLOCK, (blk + 1) * LRU_BLOCK)
        xb = xcb[:, cs]
        r = _sigmoid(jnp.dot(xb, wa_ref[blk], preferred_element_type=F32) + ba_ref[:, cs])
        gi = _sigmoid(jnp.dot(xb, wx_ref[blk], preferred_element_type=F32) + bx_ref[:, cs])
        a = jnp.exp(r * neg_c_sp[:, cs])
        y = 1.0 - a * a
        root = jnp.where(y > 0.0, y * lax.rsqrt(y), 0.0)
        a_ref[:, cs] = a
        u_ref[:, cs] = root * gi * xc[:, cs]

    def body(gi, h):
        rows = [pl.multiple_of((gi * LRU_SCAN_UNROLL + j) * nrow, nrow)
                for j in range(LRU_SCAN_UNROLL)]
        tiles = [(a_ref[pl.ds(r0, nrow), :], u_ref[pl.ds(r0, nrow), :]) for r0 in rows]
        outs = []
        for av, uv in tiles:
            for s in (1, 2, 4):
                a_sh = jnp.where(sub >= s, pltpu.roll(av, s, 0), 1.0)
                u_sh = jnp.where(sub >= s, pltpu.roll(uv, s, 0), 0.0)
                uv = av * u_sh + uv
                av = av * a_sh
            hv = uv + av * h
            h = hv[nrow - 1:nrow, :]
            outs.append(hv)
        for r0, hv in zip(rows, outs):
            u_ref[pl.ds(r0, nrow), :] = hv
        return h

    lax.fori_loop(0, seq // (nrow * LRU_SCAN_UNROLL), body, jnp.zeros((1, x.shape[1]), F32))
    o_ref[0] = (u_ref[...] * gate_ref[0].astype(F32)).astype(BF16)


def _lru(l, proj3, cw, cb, wa, ba, wx, bx, lam):
    b, seq, _ = proj3.shape
    nblk = LRU_BLOCKS_PER_STEP
    cblk = nblk * LRU_BLOCK
    vec = _layer_spec(l, (1, cblk), lambda i, h: (0, h))
    wspec = _layer_spec(l, (nblk, LRU_BLOCK, LRU_BLOCK), lambda i, h: (h, 0, 0))
    return pl.pallas_call(
        _lru_kernel,
        grid=(b, LRU_BLOCKS // nblk),
        in_specs=[
            pl.BlockSpec((1, seq, cblk), lambda i, h: (i, 0, OFF_LX // cblk + h)),
            pl.BlockSpec((1, seq, cblk), lambda i, h: (i, 0, OFF_LG // cblk + h)),
            _layer_spec(l, (CONV_WIDTH, cblk), lambda i, h: (0, h)),
            vec, wspec, vec, wspec, vec, vec,
        ],
        out_specs=pl.BlockSpec((1, seq, cblk), lambda i, h: (i, 0, h)),
        out_shape=jax.ShapeDtypeStruct((b, seq, LRU_WIDTH), BF16),
        scratch_shapes=[pltpu.VMEM((seq, cblk), F32), pltpu.VMEM((seq, cblk), F32)],
        compiler_params=_params("parallel", "parallel"),
        name="rglru",
    )(proj3, proj3, cw, cb, wa, ba, wx, bx, lam)


def _sb_kernel(q_ref, k_ref, v_ref, o_ref, acc_ref, car_ref, bias_ref):
    t, wk, d = SB_TILE, SB_WINDOW, SB_HEAD_DIM
    nt = SB_TILES_PER_STEP
    step = pl.program_id(1)
    row = lax.broadcasted_iota(jnp.int32, (t, wk), 0)
    col = lax.broadcasted_iota(jnp.int32, (t, wk), 1)
    col_minus_row = col - row
    tri = jnp.where(lax.broadcasted_iota(jnp.int32, (wk, wk), 0)
                    > lax.broadcasted_iota(jnp.int32, (wk, wk), 1), 1.0, 0.0).astype(BF16)

    def hs(h):
        return slice(h * d, (h + 1) * d)

    def rs(s):
        return slice(s * t, (s + 1) * t)

    chains = [(s, h) for s in range(nt) for h in range(SB_HEADS)]

    def window_pass(kstarts, first):
        worst = None
        for g0 in range(0, len(chains), SB_GROUP):
            group = chains[g0:g0 + SB_GROUP]
            old = None if first else [(acc_ref[s * SB_HEADS + h], car_ref[s * SB_HEADS + h])
                                      for s, h in group]
            zs = [lax.dot_general(q_ref[0, rs(s), hs(h)], k_ref[0, pl.ds(kstarts[s], wk), hs(h)],
                                  NT_DIMS, preferred_element_type=F32) + bias_ref[s]
                  for s, h in group]
            lszs = [jnp.minimum(z, 0.0) - jnp.log(1.0 + jnp.exp(-jnp.abs(z))) for z in zs]
            lks = [lsz - z for lsz, z in zip(lszs, zs)]
            afters = [jnp.dot(lk.astype(BF16), tri, preferred_element_type=F32) for lk in lks]
            carries = [jnp.sum(lk, axis=-1, keepdims=True) for lk in lks]
            if not first:
                afters = [a + o[1] for a, o in zip(afters, old)]
                carries = [c + o[1] for c, o in zip(carries, old)]
            ws = [jnp.exp(lsz + after).astype(BF16) for lsz, after in zip(lszs, afters)]
            accs = [jnp.dot(w, v_ref[0, pl.ds(kstarts[s], wk), hs(h)], preferred_element_type=F32)
                    for w, (s, h) in zip(ws, group)]
            if not first:
                accs = [a + o[0] for a, o in zip(accs, old)]
            for (s, h), acc, carry in zip(group, accs, carries):
                acc_ref[s * SB_HEADS + h] = acc
                car_ref[s * SB_HEADS + h] = carry
                if first:
                    o_ref[0, rs(s), hs(h)] = acc.astype(BF16)
                worst = carry if worst is None else jnp.maximum(worst, carry)
        return worst

    def alive(carry):
        return (jnp.max(carry) > SB_DEAD).astype(jnp.int32)

    kstarts = []
    for s in range(nt):
        tile = step * nt + s
        kstart = pl.multiple_of(jnp.maximum(tile - 1, 0) * t, t)
        bias_ref[s] = jnp.where(col_minus_row < tile * t - kstart, 0.0, SB_MASKED)
        kstarts.append(kstart)
    worst = window_pass(kstarts, True)

    @pl.when(jnp.logical_and(alive(worst) > 0, kstarts[nt - 1] > 0))
    def _():
        def cond(st):
            return jnp.logical_and(st[nt - 1] > 0, st[nt] > 0)

        def body(st):
            kends = st[:nt]
            starts = [pl.multiple_of(jnp.maximum(kend - wk, 0), t) for kend in kends]
            for s in range(nt):
                bias_ref[s] = jnp.where(col < kends[s] - starts[s], 0.0, SB_MASKED)
            return tuple(starts) + (alive(window_pass(starts, False)),)

        lax.while_loop(cond, body, tuple(kstarts) + (jnp.int32(1),))
        for s, h in chains:
            o_ref[0, rs(s), hs(h)] = acc_ref[s * SB_HEADS + h].astype(BF16)


def _sb(proj3):
    b, seq, _ = proj3.shape
    rows = SB_TILES_PER_STEP * SB_TILE
    w = SB_HEADS * SB_HEAD_DIM
    nchain = SB_TILES_PER_STEP * SB_HEADS
    return pl.pallas_call(
        _sb_kernel,
        grid=(b, seq // rows),
        in_specs=[
            pl.BlockSpec((1, rows, w), lambda i, s: (i, s, OFF_SQ // w)),
            pl.BlockSpec((1, seq, w), lambda i, s: (i, 0, OFF_SK // w)),
            pl.BlockSpec((1, seq, w), lambda i, s: (i, 0, OFF_SV // w)),
        ],
        out_specs=pl.BlockSpec((1, rows, w), lambda i, s: (i, s, 0)),
        out_shape=jax.ShapeDtypeStruct((b, seq, w), BF16),
        scratch_shapes=[pltpu.VMEM((nchain, SB_TILE, SB_HEAD_DIM), F32),
                        pltpu.VMEM((nchain, SB_TILE, 1), F32),
                        pltpu.VMEM((SB_TILES_PER_STEP, SB_TILE, SB_WINDOW), F32)],
        compiler_params=_params("parallel", "arbitrary"),
        name="stickbreak",
    )(proj3, proj3, proj3)


def _merge_kernel(x_ref, oa_ref, ob_ref, oc_ref, ga_ref, gb_ref, gc_ref, bg_ref,
                  wa_ref, wb_ref, wc_ref, wo_ref, o_ref):
    def branch(o_r, w_r, gl_r, idx):
        gate = _sigmoid(gl_r[...].astype(F32) + bg_ref[:, idx * D_MODEL:(idx + 1) * D_MODEL])
        return gate * jnp.dot(o_r[...], w_r[...].astype(BF16), preferred_element_type=F32)

    merged = (branch(oa_ref, wa_ref, ga_ref, 0) + branch(ob_ref, wb_ref, gb_ref, 1)
              + branch(oc_ref, wc_ref, gc_ref, 2))
    o_ref[...] = x_ref[...] + jnp.dot(merged.astype(BF16), wo_ref[...].astype(BF16),
                                      preferred_element_type=F32)


def _merge(l, x2, oa, ob, oc, proj, bg, wa, wb, wc, wo, tm=512):
    n = x2.shape[0]
    rows = pl.BlockSpec((tm, D_MODEL), lambda i: (i, 0))
    wspec = pl.BlockSpec((None, D_MODEL, D_MODEL), lambda i: (l, 0, 0),
                         pipeline_mode=pl.Buffered(1))
    gate_blk = OFF_GATE // D_MODEL

    def gspec(idx):
        return pl.BlockSpec((tm, D_MODEL), lambda i: (i, gate_blk + idx))

    return pl.pallas_call(
        _merge_kernel,
        grid=(n // tm,),
        in_specs=[rows, rows, rows, rows, gspec(0), gspec(1), gspec(2),
                  _layer_spec(l, (1, 3 * D_MODEL), lambda i: (0, 0)),
                  wspec, wspec, wspec, wspec],
        out_specs=rows,
        out_shape=jax.ShapeDtypeStruct((n, D_MODEL), F32),
        compiler_params=_params("parallel"),
        name="merge_out",
    )(x2, oa, ob, oc, proj, proj, proj, bg, wa, wb, wc, wo)


def _mlp_kernel(x_ref, g_ref, wu_ref, wd_ref, o_ref, hn_ref):
    @pl.when(pl.program_id(1) == 0)
    def _():
        x = x_ref[...]
        hn = x * lax.rsqrt(jnp.mean(x * x, axis=-1, keepdims=True) + EPS) * g_ref[...]
        hn_ref[...] = hn.astype(BF16)
        o_ref[...] = x

    up = jnp.dot(hn_ref[...], wu_ref[...].astype(BF16), preferred_element_type=F32)
    act = jnp.square(jnp.maximum(up, 0.0)).astype(BF16)
    o_ref[...] += jnp.dot(act, wd_ref[...].astype(BF16), preferred_element_type=F32)


def _mlp(l, x2, g, wu, wd, tm=1024, tf=1024):
    n = x2.shape[0]
    return pl.pallas_call(
        _mlp_kernel,
        grid=(n // tm, D_FF // tf),
        in_specs=[
            pl.BlockSpec((tm, D_MODEL), lambda i, f: (i, 0)),
            _layer_spec(l, (1, D_MODEL), lambda i, f: (0, 0)),
            _layer_spec(l, (D_MODEL, tf), lambda i, f: (0, f)),
            _layer_spec(l, (tf, D_MODEL), lambda i, f: (f, 0)),
        ],
        out_specs=pl.BlockSpec((tm, D_MODEL), lambda i, f: (i, 0)),
        out_shape=jax.ShapeDtypeStruct((n, D_MODEL), F32),
        scratch_shapes=[pltpu.VMEM((tm, D_MODEL), BF16)],
        compiler_params=_params("parallel", "arbitrary"),
        name="mlp",
    )(x2, g, wu, wd)


def kernel(x, norm_mix_g, w_in, gla_w_up, gla_b_alpha, gla_norm_g, lru_conv_w, lru_conv_b, lru_w_a,
           lru_b_a, lru_w_x, lru_b_x, lru_lambda, sb_q_norm_g, sb_k_norm_g, w_branch_a, w_branch_b,
           w_branch_c, b_gate, w_out, norm_mlp_g, w_mlp_up, w_mlp_down):
    b, seq, d = x.shape
    depth = w_in.shape[0]
    ad0 = 2 * GLA_DK + 2 * GLA_DV
    w_lo = w_in[:, :, :ad0].astype(BF16)
    w_hi = w_in[:, :, ad0 + GLA_RANK:].astype(BF16)
    w_ad = jnp.pad(w_in[:, :, ad0:ad0 + GLA_RANK], ((0, 0), (0, 0), (0, LANE - GLA_RANK))).astype(BF16)
    wup = jnp.pad(gla_w_up, ((0, 0), (0, LANE - GLA_RANK), (0, 0))).astype(BF16)
    wa_l, wx_l = lru_w_a.astype(BF16), lru_w_x.astype(BF16)
    wba, wbb, wbc, wo, wu, wd = w_branch_a, w_branch_b, w_branch_c, w_out, w_mlp_up, w_mlp_down

    def vec(p):
        return p[:, None, :]

    mix_g, mlp_g = vec(norm_mix_g), vec(norm_mlp_g)
    b_alpha, gla_ng = vec(gla_b_alpha), vec(gla_norm_g)
    conv_b, b_a, b_x, lam = vec(lru_conv_b), vec(lru_b_a), vec(lru_b_x), vec(lru_lambda)

    ep = jnp.zeros((depth, PROJ_COLS // D_MODEL, D_MODEL), F32)
    ep = ep.at[:, TILE_SQ].set(jnp.tile(sb_q_norm_g * SB_HEAD_DIM ** -0.5, (1, SB_HEADS)))
    ep = ep.at[:, TILE_SK].set(jnp.tile(sb_k_norm_g, (1, SB_HEADS)))
    ep = ep[:, :, None, :]
    bg = vec(b_gate)

    x2 = x.reshape(b * seq, d)
    for l in range(depth):
        proj, ad = _in_proj(l, x2, mix_g, w_lo, w_hi, w_ad, ep)
        proj3 = proj.reshape(b, seq, PROJ_COLS)
        ad3 = ad.reshape(b, seq, LANE)
        o_a = _gla(l, proj3, ad3, wup, b_alpha, gla_ng)
        o_b = _lru(l, proj3, lru_conv_w, conv_b, wa_l, b_a, wx_l, b_x, lam)
        o_c = _sb(proj3)
        x2 = _merge(l, x2, o_a.reshape(b * seq, d), o_b.reshape(b * seq, d), o_c.reshape(b * seq, d),
                    proj, bg, wba, wbb, wbc, wo)
        x2 = _mlp(l, x2, mlp_g, wu, wd)
    return x2.reshape(b, seq, d)
```
